```python
import jax, jax.numpy as jnp
from jax import lax
import numpy as np

D_MODEL = 2048
BATCH = 1
SEQ = 16384
DEPTH = 4

CHUNK = 64
N_A_LAYERS = DEPTH - DEPTH // 2
N_B_LAYERS = DEPTH // 2

GDN_HEADS = 16
GDN_HEAD_DIM = 128
GDN_QK_DIM = GDN_HEADS * GDN_HEAD_DIM
GDN_V_DIM = GDN_HEADS * GDN_HEAD_DIM
GDN_QKV_DIM = 2 * GDN_QK_DIM + GDN_V_DIM
GDN_PROJ_DIM = GDN_QKV_DIM + GDN_V_DIM + 2 * GDN_HEADS
GDN_CONV = 4

SB_HEADS = 16
SB_HEAD_DIM = 128
SB_DIM = SB_HEADS * SB_HEAD_DIM
SB_BLOCK = 128

D_FF = 5632
FFN_CONV = 3
EPS = 1e-6

kernel_name = "yoco_gdn_stickbreak_convglu"


def rms_norm(x, w):
    xf = x.astype(jnp.float32)
    y = xf * lax.rsqrt(jnp.mean(xf * xf, axis=-1, keepdims=True) + EPS)
    return (y * w.astype(jnp.float32)).astype(x.dtype)


def causal_dwconv(x, w):
    width, ch = w.shape
    return lax.conv_general_dilated(
        x, w[:, None, :].astype(x.dtype), window_strides=(1,), padding=[(width - 1, 0)],
        dimension_numbers=("NWC", "WIO", "NWC"), feature_group_count=ch)


def l2norm(x):
    return x * lax.rsqrt(jnp.sum(x * x, axis=-1, keepdims=True) + EPS)


def gated_delta_chunked(q, k, v, g, beta):
    B, S, H, dk = q.shape
    dv = v.shape[-1]
    n = S // CHUNK

    def to_chunks(t):
        return jnp.moveaxis(t.reshape((B, n, CHUNK, H) + t.shape[3:]), 3, 1)

    q, k, v, g, beta = (to_chunks(t) for t in (q, k, v, g, beta))
    q = q * dk ** -0.5
    G = jnp.cumsum(g, axis=-1)
    idx = jnp.arange(CHUNK)
    causal = idx[:, None] >= idx[None, :]
    strict = idx[:, None] > idx[None, :]
    decay = jnp.exp(jnp.where(causal, G[..., :, None] - G[..., None, :], -jnp.inf))
    k_beta = k * beta[..., None]
    m = jnp.where(strict, jnp.einsum("bhncd,bhnsd->bhncs", k_beta, k) * decay, 0.0)
    lower = m + jnp.eye(CHUNK, dtype=jnp.float32)
    rhs = jnp.concatenate([v * beta[..., None], k_beta * jnp.exp(G)[..., None]], axis=-1)
    sol = lax.linalg.triangular_solve(lower, rhs, left_side=True, lower=True, unit_diagonal=True)
    u, w = sol[..., :dv], sol[..., dv:]
    attn_qk = jnp.where(causal, jnp.einsum("bhncd,bhnsd->bhncs", q, k) * decay, 0.0)
    q_dec = q * jnp.exp(G)[..., None]
    k_dec = k * jnp.exp(G[..., -1:] - G)[..., None]
    g_last = jnp.exp(G[..., -1])

    def step(state, inp):
        u_c, w_c, qd_c, kd_c, a_c, gl_c = inp
        v_new = u_c - jnp.einsum("bhcd,bhde->bhce", w_c, state)
        o = jnp.einsum("bhcd,bhde->bhce", qd_c, state) + jnp.einsum("bhcs,bhse->bhce", a_c, v_new)
        state = state * gl_c[..., None, None] + jnp.einsum("bhcd,bhce->bhde", kd_c, v_new)
        return state, o

    xs = tuple(jnp.moveaxis(t, 2, 0) for t in (u, w, q_dec, k_dec, attn_qk, g_last))
    s0 = jnp.zeros((B, H, dk, dv), jnp.float32)
    _, o = lax.scan(step, s0, xs)
    o = jnp.moveaxis(o, 0, 2)
    return jnp.moveaxis(o, 1, 3).reshape(B, S, H, dv)


def gdn_mixer(h, w_in, conv_w, a_log, dt_bias, o_norm_w, w_out):
    B, S, _ = h.shape
    proj = h @ w_in
    qkv, gate, a, b = jnp.split(proj, [GDN_QKV_DIM, GDN_QKV_DIM + GDN_V_DIM,
                                       GDN_QKV_DIM + GDN_V_DIM + GDN_HEADS], axis=-1)
    qkv = jax.nn.silu(causal_dwconv(qkv, conv_w)).astype(jnp.float32)
    q, k, v = jnp.split(qkv, [GDN_QK_DIM, 2 * GDN_QK_DIM], axis=-1)
    q = l2norm(q.reshape(B, S, GDN_HEADS, GDN_HEAD_DIM))
    k = l2norm(k.reshape(B, S, GDN_HEADS, GDN_HEAD_DIM))
    v = v.reshape(B, S, GDN_HEADS, GDN_HEAD_DIM)
    beta = jax.nn.sigmoid(b.astype(jnp.float32))
    g = -jnp.exp(a_log.astype(jnp.float32)) * jax.nn.softplus(a.astype(jnp.float32) + dt_bias.astype(jnp.float32))
    o = gated_delta_chunked(q, k, v, g, beta)
    o = rms_norm(o, o_norm_w) * jax.nn.silu(gate.astype(jnp.float32)).reshape(B, S, GDN_HEADS, GDN_HEAD_DIM)
    return o.reshape(B, S, GDN_V_DIM).astype(h.dtype) @ w_out


def stick_breaking_attention(q, k, v):
    B, H, S, d = q.shape
    nb = S // SB_BLOCK
    idx = jnp.arange(SB_BLOCK)
    scale = d ** -0.5

    def per_query_block(qb):
        q_blk = lax.dynamic_slice_in_dim(q, qb * SB_BLOCK, SB_BLOCK, axis=2) * scale
        t_pos = qb * SB_BLOCK + idx

        def body(i, carry):
            acc, log_surv = carry
            kb = qb - i
            k_blk = lax.dynamic_slice_in_dim(k, kb * SB_BLOCK, SB_BLOCK, axis=2)
            v_blk = lax.dynamic_slice_in_dim(v, kb * SB_BLOCK, SB_BLOCK, axis=2)
            valid = (kb * SB_BLOCK + idx)[None, :] < t_pos[:, None]
            z = jnp.einsum("bhqd,bhkd->bhqk", q_blk, k_blk)
            log_one_minus = jnp.where(valid, jax.nn.log_sigmoid(-z), 0.0)
            incl = jnp.flip(jnp.cumsum(jnp.flip(log_one_minus, -1), axis=-1), -1)
            excl = jnp.concatenate([incl[..., 1:], jnp.zeros_like(incl[..., :1])], axis=-1)
            log_a = jnp.where(valid, jax.nn.log_sigmoid(z) + excl + log_surv[..., None], -jnp.inf)
            acc = acc + jnp.einsum("bhqk,bhkd->bhqd", jnp.exp(log_a), v_blk)
            return acc, log_surv + incl[..., 0]

        init = (jnp.zeros((B, H, SB_BLOCK, d), jnp.float32), jnp.zeros((B, H, SB_BLOCK), jnp.float32))
        acc, _ = lax.fori_loop(0, qb + 1, body, init)
        return acc

    out = lax.map(per_query_block, jnp.arange(nb))
    return jnp.moveaxis(out, 0, 2).reshape(B, H, S, d)


def shared_kv(h, w_kv):
    B, S, _ = h.shape
    kv = (h @ w_kv).astype(jnp.float32)
    k, v = jnp.split(kv, 2, axis=-1)
    k = k.reshape(B, S, SB_HEADS, SB_HEAD_DIM).transpose(0, 2, 1, 3)
    v = v.reshape(B, S, SB_HEADS, SB_HEAD_DIM).transpose(0, 2, 1, 3)
    return k, v


def sb_mixer(h, w_q, k, v, w_out):
    B, S, _ = h.shape
    q = (h @ w_q).astype(jnp.float32).reshape(B, S, SB_HEADS, SB_HEAD_DIM).transpose(0, 2, 1, 3)
    o = stick_breaking_attention(q, k, v)
    return o.transpose(0, 2, 1, 3).reshape(B, S, SB_DIM).astype(h.dtype) @ w_out


def conv_ffn(h, w_up, conv_w, conv_b, w_down):
    u = causal_dwconv(h @ w_up, conv_w) + conv_b
    gate, val = jnp.split(u, 2, axis=-1)
    return (jax.nn.silu(gate) * val) @ w_down


def setup_inputs(seed: int = 0) -> dict:
    key = jax.random.key(seed)
    ks = jax.random.split(key, 20)
    f32 = jnp.float32

    def nrm(k, shape, fan_in):
        return jax.random.normal(k, shape, f32) * (fan_in ** -0.5)

    def gain(k, shape):
        return 1.0 + 0.02 * jax.random.normal(k, shape, f32)

    dt = jnp.exp(jax.random.uniform(ks[5], (N_A_LAYERS, GDN_HEADS), f32) * (np.log(0.1) - np.log(0.001)) + np.log(0.001))
    return {
        "x": jax.random.normal(ks[0], (BATCH, SEQ, D_MODEL), f32),
        "ln_mix_w": gain(ks[1], (DEPTH, D_MODEL)),
        "ln_ffn_w": gain(ks[2], (DEPTH, D_MODEL)),
        "gdn_w_in": nrm(ks[3], (N_A_LAYERS, D_MODEL, GDN_PROJ_DIM), D_MODEL),
        "gdn_conv_w": nrm(ks[4], (N_A_LAYERS, GDN_CONV, GDN_QKV_DIM), GDN_CONV),
        "gdn_a_log": jnp.log(jax.random.uniform(ks[6], (N_A_LAYERS, GDN_HEADS), f32, 1.0, 16.0)),
        "gdn_dt_bias": dt + jnp.log(-jnp.expm1(-dt)),
        "gdn_o_norm_w": gain(ks[7], (N_A_LAYERS, GDN_HEAD_DIM)),
        "gdn_w_out": nrm(ks[8], (N_A_LAYERS, GDN_V_DIM, D_MODEL), GDN_V_DIM),
        "kv_norm_w": gain(ks[9], (D_MODEL,)),
        "sb_w_kv": nrm(ks[10], (D_MODEL, 2 * SB_DIM), D_MODEL),
        "sb_w_q": nrm(ks[11], (N_B_LAYERS, D_MODEL, SB_DIM), D_MODEL),
        "sb_w_out": nrm(ks[12], (N_B_LAYERS, SB_DIM, D_MODEL), SB_DIM),
        "ffn_w_up": nrm(ks[13], (DEPTH, D_MODEL, 2 * D_FF), D_MODEL),
        "ffn_conv_w": nrm(ks[14], (DEPTH, FFN_CONV, 2 * D_FF), FFN_CONV),
        "ffn_conv_b": 0.01 * jax.random.normal(ks[15], (DEPTH, 2 * D_FF), f32),
        "ffn_w_down": nrm(ks[16], (DEPTH, D_FF, D_MODEL), D_FF),
        "final_norm_w": gain(ks[17], (D_MODEL,)),
    }


def reference(x, ln_mix_w, ln_ffn_w, gdn_w_in, gdn_conv_w, gdn_a_log, gdn_dt_bias, gdn_o_norm_w,
              gdn_w_out, kv_norm_w, sb_w_kv, sb_w_q, sb_w_out, ffn_w_up, ffn_conv_w, ffn_conv_b,
              ffn_w_down, final_norm_w):
    h = x
    k_shared, v_shared = None, None
    for layer in range(DEPTH):
        hn = rms_norm(h, ln_mix_w[layer])
        if layer < N_A_LAYERS:
            i = layer
            h = h + gdn_mixer(hn, gdn_w_in[i], gdn_conv_w[i], gdn_a_log[i], gdn_dt_bias[i],
                              gdn_o_norm_w[i], gdn_w_out[i])
        else:
            j = layer - N_A_LAYERS
            h = h + sb_mixer(hn, sb_w_q[j], k_shared, v_shared, sb_w_out[j])
        h = h + conv_ffn(rms_norm(h, ln_ffn_w[layer]), ffn_w_up[layer], ffn_conv_w[layer],
                         ffn_conv_b[layer], ffn_w_down[layer])
        if layer == N_A_LAYERS - 1:
            k_shared, v_shared = shared_kv(rms_norm(h, kv_norm_w), sb_w_kv)
    return rms_norm(h, final_norm_w)
```

```python
import functools

import jax
import jax.numpy as jnp
from jax import lax
from jax.experimental import pallas as pl
from jax.experimental.pallas import tpu as pltpu

F32 = jnp.float32
BF16 = jnp.bfloat16
EPS = 1e-6

D_MODEL = 2048
N_HEADS = 16
HEAD_DIM = 128
D_FF = 5632
GDN_CHUNK = 64
GDN_BLOCK = 256
GDN_HEADS_PER_STEP = 2
SB_BLOCK = 256
SB_LOG_SURV_FLOOR = -104.0
VMEM_LIMIT = 56 * 1024 * 1024


def _params(sem):
    return pltpu.CompilerParams(dimension_semantics=sem, vmem_limit_bytes=VMEM_LIMIT)


def _dot(a, b):
    return jnp.dot(a, b, preferred_element_type=F32)


def _dot_nt(a, b):
    return lax.dot_general(a, b, (((1,), (1,)), ((), ())), preferred_element_type=F32)


def _dot_tn(a, b):
    return lax.dot_general(a, b, (((0,), (0,)), ((), ())), preferred_element_type=F32)


def _split3(a):
    hi = a.astype(BF16)
    r1 = a - hi.astype(F32)
    mid = r1.astype(BF16)
    lo = (r1 - mid.astype(F32)).astype(BF16)
    return hi, mid, lo


def _sigmoid(x):
    return 1.0 / (1.0 + jnp.exp(-x))


def _softplus(x):
    return jnp.maximum(x, 0.0) + jnp.log(1.0 + jnp.exp(-jnp.abs(x)))


def _rms(x, w):
    return x * lax.rsqrt(jnp.mean(x * x, axis=-1, keepdims=True) + EPS) * w


def _rmsnorm_kernel(x_ref, w_ref, o_ref):
    o_ref[...] = _rms(x_ref[...], w_ref[...]).astype(o_ref.dtype)


def rmsnorm(x, w, out_dtype, tm=512):
    s, d = x.shape
    return pl.pallas_call(
        _rmsnorm_kernel,
        grid=(s // tm,),
        in_specs=[pl.BlockSpec((tm, d), lambda i: (i, 0)), pl.BlockSpec((1, d), lambda i: (0, 0))],
        out_specs=pl.BlockSpec((tm, d), lambda i: (i, 0)),
        out_shape=jax.ShapeDtypeStruct((s, d), out_dtype),
        compiler_params=_params(("parallel",)),
        name="rmsnorm",
    )(x, w.reshape(1, d))


def _matmul_kernel(x_ref, w_ref, o_ref):
    o_ref[...] = _dot(x_ref[...], w_ref[...]).astype(o_ref.dtype)


def matmul(x, w, out_dtype, tm=1024, tn=1024):
    s, k = x.shape
    n = w.shape[1]
    return pl.pallas_call(
        _matmul_kernel,
        grid=(s // tm, n // tn),
        in_specs=[pl.BlockSpec((tm, k), lambda i, j: (i, 0)), pl.BlockSpec((k, tn), lambda i, j: (0, j))],
        out_specs=pl.BlockSpec((tm, tn), lambda i, j: (i, j)),
        out_shape=jax.ShapeDtypeStruct((s, n), out_dtype),
        compiler_params=_params(("parallel", "parallel")),
        name="matmul",
    )(x, w)


def _matmul_heads_kernel(x_ref, w_ref, o_ref, *, scale, heads_per_tile):
    acc = _dot(x_ref[...], w_ref[...])
    if scale is not None:
        acc = acc * scale
    for t in range(heads_per_tile):
        o_ref[t] = acc[:, t * HEAD_DIM:(t + 1) * HEAD_DIM].astype(o_ref.dtype)


def matmul_heads(x, w, scale=None, tm=1024, heads_per_tile=4):
    s, k = x.shape
    nh = w.shape[1] // HEAD_DIM
    tn = heads_per_tile * HEAD_DIM
    return pl.pallas_call(
        functools.partial(_matmul_heads_kernel, scale=scale, heads_per_tile=heads_per_tile),
        grid=(s // tm, nh // heads_per_tile),
        in_specs=[pl.BlockSpec((tm, k), lambda i, j: (i, 0)), pl.BlockSpec((k, tn), lambda i, j: (0, j))],
        out_specs=pl.BlockSpec((heads_per_tile, tm, HEAD_DIM), lambda i, j: (j, i, 0)),
        out_shape=jax.ShapeDtypeStruct((nh, s, HEAD_DIM), BF16),
        compiler_params=_params(("parallel", "parallel")),
        name="matmul_heads",
    )(x, w)


def _proj_resid_norm_kernel(a_ref, w_ref, r_ref, nw_ref, h_ref, hn_ref):
    h = r_ref[...] + _dot(a_ref[...], w_ref[...])
    h_ref[...] = h
    hn_ref[...] = _rms(h, nw_ref[...]).astype(hn_ref.dtype)


def proj_resid_norm(a, w, resid, norm_w, tm=256):
    s, k = a.shape
    d = w.shape[1]
    return pl.pallas_call(
        _proj_resid_norm_kernel,
        grid=(s // tm,),
        in_specs=[pl.BlockSpec((tm, k), lambda i: (i, 0)), pl.BlockSpec((k, d), lambda i: (0, 0)),
                  pl.BlockSpec((tm, d), lambda i: (i, 0)), pl.BlockSpec((1, d), lambda i: (0, 0))],
        out_specs=[pl.BlockSpec((tm, d), lambda i: (i, 0)), pl.BlockSpec((tm, d), lambda i: (i, 0))],
        out_shape=[jax.ShapeDtypeStruct((s, d), F32), jax.ShapeDtypeStruct((s, d), BF16)],
        compiler_params=_params(("parallel",)),
        name="proj_resid_norm",
    )(a, w, resid, norm_w.reshape(1, d))


def _chunk_masks(tb, chunk):
    shift = chunk.bit_length() - 1
    row = lax.broadcasted_iota(jnp.int32, (tb, tb), 0)
    col = lax.broadcasted_iota(jnp.int32, (tb, tb), 1)
    same = lax.shift_right_logical(row, shift) == lax.shift_right_logical(col, shift)
    return jnp.logical_and(same, row >= col), jnp.logical_and(same, row > col)


def _gdn_gates_kernel(hn_ref, wa_ref, wb_ref, alog_ref, dtb_ref, g_ref, gt_ref, beta_ref, *, tb, chunk):
    hn = hn_ref[...]
    a = _dot(hn, wa_ref[...])
    b = _dot(hn, wb_ref[...])
    g = -jnp.exp(alog_ref[...]) * _softplus(a + dtb_ref[...])
    causal, _ = _chunk_masks(tb, chunk)
    lower = causal.astype(BF16)
    g_hi, g_mid, g_lo = _split3(g)
    gcum = _dot(lower, g_hi) + _dot(lower, g_mid) + _dot(lower, g_lo)
    g_ref[...] = gcum
    gt_ref[...] = gcum.T[:N_HEADS, :]
    beta_ref[...] = _sigmoid(b)


def gdn_gates(hn, w_a, w_b, a_log, dt_bias, tb=GDN_BLOCK, chunk=GDN_CHUNK):
    s, d = hn.shape
    pad = lambda v: jnp.pad(v.astype(F32), (0, 128 - N_HEADS)).reshape(1, 128)
    return pl.pallas_call(
        functools.partial(_gdn_gates_kernel, tb=tb, chunk=chunk),
        grid=(s // tb,),
        in_specs=[pl.BlockSpec((tb, d), lambda i: (i, 0)),
                  pl.BlockSpec((d, 128), lambda i: (0, 0)), pl.BlockSpec((d, 128), lambda i: (0, 0)),
                  pl.BlockSpec((1, 128), lambda i: (0, 0)), pl.BlockSpec((1, 128), lambda i: (0, 0))],
        out_specs=[pl.BlockSpec((tb, 128), lambda i: (i, 0)),
                   pl.BlockSpec((N_HEADS, tb), lambda i: (0, i)),
                   pl.BlockSpec((tb, 128), lambda i: (i, 0))],
        out_shape=[jax.ShapeDtypeStruct((s, 128), F32), jax.ShapeDtypeStruct((N_HEADS, s), F32),
                   jax.ShapeDtypeStruct((s, 128), F32)],
        compiler_params=_params(("parallel",)),
        name="gdn_gates",
    )(hn, w_a, w_b, pad(a_log), pad(dt_bias))


def _split2(a):
    hi = a.astype(BF16)
    return hi, (a - hi.astype(F32)).astype(BF16)


def _mm_hi(a, b):
    ah, al = _split2(a)
    bh, bl = _split2(b)
    return _dot(ah, bh) + (_dot(ah, bl) + _dot(al, bh))


def _neumann(n_mat, levels):
    x = n_mat
    q = n_mat
    for _ in range(levels):
        q = _mm_hi(q, q)
        x = x + q + _mm_hi(x, q)
    return x


def _gdn_core_kernel(pq_ref, pk_ref, pv_ref, pg_ref, cq_ref, ck_ref, cv_ref, g_ref, gt_ref, beta_ref, onw_ref,
                     o_ref, xq, xk, xv, state, *, tb, chunk, hb):
    hg = pl.program_id(0)
    t = pl.program_id(1)

    @pl.when(t == 0)
    def _():
        state[...] = jnp.zeros_like(state)
        for scr in (xq, xk, xv):
            scr[0:8, :] = jnp.zeros((8, hb * HEAD_DIM), F32)

    def conv_silu(p_ref, cw_ref, scr):
        scr[8:8 + tb, :] = p_ref[...]
        cw = cw_ref[...]
        y = (cw[0:1, :] * scr[5:5 + tb, :] + cw[1:2, :] * scr[6:6 + tb, :]
             + cw[2:3, :] * scr[7:7 + tb, :] + cw[3:4, :] * scr[8:8 + tb, :])
        scr[0:8, :] = scr[tb:tb + 8, :]
        return y * _sigmoid(y)

    yq = conv_silu(pq_ref, cq_ref, xq)
    yk = conv_silu(pk_ref, ck_ref, xk)
    yv = conv_silu(pv_ref, cv_ref, xv)

    gblk = g_ref[...]
    bblk = beta_ref[...]
    lane = lax.broadcasted_iota(jnp.int32, (tb, 128), 1)
    causal, strict = _chunk_masks(tb, chunk)
    levels = chunk.bit_length() - 2

    for hh in range(hb):
        sl = slice(hh * HEAD_DIM, (hh + 1) * HEAD_DIM)
        head = hg * hb + hh
        q = yq[:, sl]
        k = yk[:, sl]
        v = yv[:, sl]
        q = q * lax.rsqrt(jnp.sum(q * q, axis=-1, keepdims=True) + EPS) * (HEAD_DIM ** -0.5)
        k = k * lax.rsqrt(jnp.sum(k * k, axis=-1, keepdims=True) + EPS)
        g_col = jnp.sum(jnp.where(lane == head, gblk, 0.0), axis=1, keepdims=True)
        b_col = jnp.sum(jnp.where(lane == head, bblk, 0.0), axis=1, keepdims=True)
        g_row = gt_ref[pl.ds(head, 1), :]
        e_g = jnp.exp(g_col)

        k_beta = k * b_col
        k_bf = k.astype(BF16)
        decay = jnp.where(causal, jnp.exp(jnp.minimum(g_col - g_row, 0.0)), 0.0)
        a_mat = _dot_nt(k_beta.astype(BF16), k_bf)
        n_mat = jnp.where(strict, -(a_mat * decay), 0.0)
        x_mat = _neumann(n_mat, levels)
        rhs = jnp.concatenate([v * b_col, k_beta * e_g], axis=1)
        sol = rhs + _mm_hi(x_mat, rhs)
        u = sol[:, :HEAD_DIM]
        w_bf = sol[:, HEAD_DIM:].astype(BF16)
        attn = _dot_nt(q.astype(BF16), k_bf) * decay
        q_dec_bf = (q * e_g).astype(BF16)

        s = state[hh]
        v_new_parts = []
        o_state_parts = []
        for j in range(tb // chunk):
            r = slice(j * chunk, (j + 1) * chunk)
            s_bf = s.astype(BF16)
            v_new_j = (u[r] - _dot(w_bf[r], s_bf)).astype(BF16)
            o_state_parts.append(_dot(q_dec_bf[r], s_bf))
            g_last = g_col[(j + 1) * chunk - 1:(j + 1) * chunk, :]
            k_dec = k[r] * jnp.exp(g_last - g_col[r])
            s = s * jnp.exp(g_last) + _dot_tn(k_dec.astype(BF16), v_new_j)
            v_new_parts.append(v_new_j)
        state[hh] = s
        v_new_bf = jnp.concatenate(v_new_parts, axis=0)
        o = jnp.concatenate(o_state_parts, axis=0) + _dot(attn.astype(BF16), v_new_bf)

        gate = pg_ref[:, sl]
        o_ref[:, sl] = (_rms(o, onw_ref[...]) * (gate * _sigmoid(gate))).astype(o_ref.dtype)


def gdn_core(proj, gcum, gcum_t, beta, conv_w, o_norm_w, tb=GDN_BLOCK, chunk=GDN_CHUNK, hb=GDN_HEADS_PER_STEP):
    s = proj.shape[0]
    w = hb * HEAD_DIM
    nhg = N_HEADS // hb
    pspec = lambda off: pl.BlockSpec((tb, w), lambda h, t, off=off: (t, off * nhg + h))
    cspec = lambda off: pl.BlockSpec((4, w), lambda h, t, off=off: (0, off * nhg + h))
    return pl.pallas_call(
        functools.partial(_gdn_core_kernel, tb=tb, chunk=chunk, hb=hb),
        grid=(nhg, s // tb),
        in_specs=[pspec(0), pspec(1), pspec(2), pspec(3), cspec(0), cspec(1), cspec(2),
                  pl.BlockSpec((tb, 128), lambda h, t: (t, 0)),
                  pl.BlockSpec((N_HEADS, tb), lambda h, t: (0, t)),
                  pl.BlockSpec((tb, 128), lambda h, t: (t, 0)),
                  pl.BlockSpec((1, HEAD_DIM), lambda h, t: (0, 0))],
        out_specs=pl.BlockSpec((tb, w), lambda h, t: (t, h)),
        out_shape=jax.ShapeDtypeStruct((s, N_HEADS * HEAD_DIM), BF16),
        scratch_shapes=[pltpu.VMEM((tb + 8, w), F32), pltpu.VMEM((tb + 8, w), F32),
                        pltpu.VMEM((tb + 8, w), F32), pltpu.VMEM((hb, HEAD_DIM, HEAD_DIM), F32)],
        compiler_params=_params(("parallel", "arbitrary")),
        name="gdn_core",
    )(proj, proj, proj, proj, conv_w, conv_w, conv_w, gcum, gcum_t, beta, o_norm_w.reshape(1, HEAD_DIM))


def _sb_kernel(q_ref, k_ref, v_ref, o_ref, acc, ls, *, blk):
    i = pl.program_id(1)
    q = q_ref[0]
    row = lax.broadcasted_iota(jnp.int32, (blk, blk), 0)
    col = lax.broadcasted_iota(jnp.int32, (blk, blk), 1)
    suffix = (row >= col).astype(BF16)
    acc[...] = jnp.zeros_like(acc)
    ls[...] = jnp.zeros_like(ls)

    def sweep(kb, diagonal):
        off = pl.multiple_of(kb * blk, blk)
        k = k_ref[0, pl.ds(off, blk), :]
        v = v_ref[0, pl.ds(off, blk), :]
        z = _dot_nt(q, k)
        log_1m = -_softplus(z)
        if diagonal:
            valid = col < row
            log_1m = jnp.where(valid, log_1m, 0.0)
        l_hi, l_mid, l_lo = _split3(log_1m)
        incl = _dot(l_hi, suffix) + _dot(l_mid, suffix) + _dot(l_lo, suffix)
        p = jnp.exp(z + incl + ls[...])
        if diagonal:
            p = jnp.where(valid, p, 0.0)
        acc[...] += _dot(p.astype(BF16), v)
        ls_new = ls[...] + incl[:, 0:1]
        ls[...] = ls_new
        return jnp.max(ls_new)

    m0 = sweep(i, True)

    def cond(carry):
        kb, m = carry
        return jnp.logical_and(kb >= 0, m > SB_LOG_SURV_FLOOR)

    def body(carry):
        kb, _ = carry
        return kb - 1, sweep(kb, False)

    lax.while_loop(cond, body, (i - 1, m0))
    o_ref[...] = acc[...].astype(o_ref.dtype)


def sb_attention(q, kv, blk=SB_BLOCK):
    nh, s, d = q.shape
    return pl.pallas_call(
        functools.partial(_sb_kernel, blk=blk),
        grid=(nh, s // blk),
        in_specs=[pl.BlockSpec((1, blk, d), lambda h, i: (h, i, 0)),
                  pl.BlockSpec((1, s, d), lambda h, i: (h, 0, 0)),
                  pl.BlockSpec((1, s, d), lambda h, i: (nh + h, 0, 0))],
        out_specs=pl.BlockSpec((blk, d), lambda h, i: (i, h)),
        out_shape=jax.ShapeDtypeStruct((s, nh * d), BF16),
        scratch_shapes=[pltpu.VMEM((blk, d), F32), pltpu.VMEM((blk, 1), F32)],
        compiler_params=_params(("parallel", "arbitrary")),
        name="sb_attention",
    )(q, kv, kv)


FFN_HALO = 16


def _ffn_kernel(x_ref, halo_ref, wg_ref, wv_ref, cwg_ref, cwv_ref, cbg_ref, cbv_ref, wd_ref, r_ref, nw_ref,
                h_ref, hn_ref, ug, uv, *, tm, nj):
    j = pl.program_id(1)
    x = x_ref[...]
    halo = halo_ref[0]

    def conv_branch(w_ref, cw_ref, cb_ref, u):
        w = w_ref[...]
        u[0:FFN_HALO, :] = _dot(halo, w)
        u[FFN_HALO:FFN_HALO + tm, :] = _dot(x, w)
        cw = cw_ref[...]
        return (cw[0:1, :] * u[FFN_HALO - 2:FFN_HALO - 2 + tm, :] + cw[1:2, :] * u[FFN_HALO - 1:FFN_HALO - 1 + tm, :]
                + cw[2:3, :] * u[FFN_HALO:FFN_HALO + tm, :] + cb_ref[...])

    yg = conv_branch(wg_ref, cwg_ref, cbg_ref, ug)
    yv = conv_branch(wv_ref, cwv_ref, cbv_ref, uv)
    act = (yg * _sigmoid(yg)) * yv
    contrib = _dot(act.astype(BF16), wd_ref[...])

    @pl.when(j == 0)
    def _():
        h_ref[...] = r_ref[...] + contrib

    @pl.when(j > 0)
    def _():
        h_ref[...] += contrib

    @pl.when(j == nj - 1)
    def _():
        hn_ref[...] = _rms(h_ref[...], nw_ref[...]).astype(hn_ref.dtype)


def conv_ffn(hn, w_up, conv_w, conv_b, w_down, resid, norm_w, norm_dtype, tm=512, tf=512):
    s, d = hn.shape
    nt = s // tm
    nj = D_FF // tf
    tails = hn.reshape(nt, tm, d)[:, tm - FFN_HALO:, :]
    halo = jnp.concatenate([jnp.zeros((1, FFN_HALO, d), hn.dtype), tails[:-1]], axis=0)
    conv_b = conv_b.reshape(1, 2 * D_FF)
    return pl.pallas_call(
        functools.partial(_ffn_kernel, tm=tm, nj=nj),
        grid=(nt, nj),
        in_specs=[pl.BlockSpec((tm, d), lambda i, j: (i, 0)),
                  pl.BlockSpec((1, FFN_HALO, d), lambda i, j: (i, 0, 0)),
                  pl.BlockSpec((d, tf), lambda i, j: (0, j)),
                  pl.BlockSpec((d, tf), lambda i, j: (0, nj + j)),
                  pl.BlockSpec((3, tf), lambda i, j: (0, j)),
                  pl.BlockSpec((3, tf), lambda i, j: (0, nj + j)),
                  pl.BlockSpec((1, tf), lambda i, j: (0, j)),
                  pl.BlockSpec((1, tf), lambda i, j: (0, nj + j)),
                  pl.BlockSpec((tf, d), lambda i, j: (j, 0)),
                  pl.BlockSpec((tm, d), lambda i, j: (i, 0)),
                  pl.BlockSpec((1, d), lambda i, j: (0, 0))],
        out_specs=[pl.BlockSpec((tm, d), lambda i, j: (i, 0)), pl.BlockSpec((tm, d), lambda i, j: (i, 0))],
        out_shape=[jax.ShapeDtypeStruct((s, d), F32), jax.ShapeDtypeStruct((s, d), norm_dtype)],
        scratch_shapes=[pltpu.VMEM((tm + FFN_HALO, tf), F32), pltpu.VMEM((tm + FFN_HALO, tf), F32)],
        compiler_params=_params(("parallel", "arbitrary")),
        name="conv_ffn",
    )(hn, halo, w_up, w_up, conv_w, conv_w, conv_b, conv_b, w_down, resid, norm_w.reshape(1, d))


def kernel(x, ln_mix_w, ln_ffn_w, gdn_w_in, gdn_conv_w, gdn_a_log, gdn_dt_bias, gdn_o_norm_w, gdn_w_out,
           kv_norm_w, sb_w_kv, sb_w_q, sb_w_out, ffn_w_up, ffn_conv_w, ffn_conv_b, ffn_w_down, final_norm_w):
    depth = ln_mix_w.shape[0]
    n_gdn = gdn_w_in.shape[0]
    qkvg = 4 * N_HEADS * HEAD_DIM
    h = x[0]
    hn = rmsnorm(h, ln_mix_w[0], BF16)
    kv = None
    for layer in range(depth):
        if layer < n_gdn:
            w_in = gdn_w_in[layer]
            pad_w = lambda w: jnp.pad(w, ((0, 0), (0, 128 - N_HEADS))).astype(BF16)
            proj = matmul(hn, w_in[:, :qkvg].astype(BF16), F32)
            gcum, gcum_t, beta = gdn_gates(hn, pad_w(w_in[:, qkvg:qkvg + N_HEADS]), pad_w(w_in[:, qkvg + N_HEADS:]),
                                           gdn_a_log[layer], gdn_dt_bias[layer])
            o = gdn_core(proj, gcum, gcum_t, beta, gdn_conv_w[layer], gdn_o_norm_w[layer])
            h, hn = proj_resid_norm(o, gdn_w_out[layer].astype(BF16), h, ln_ffn_w[layer])
        else:
            j = layer - n_gdn
            q = matmul_heads(hn, sb_w_q[j].astype(BF16), scale=HEAD_DIM ** -0.5)
            o = sb_attention(q, kv)
            h, hn = proj_resid_norm(o, sb_w_out[j].astype(BF16), h, ln_ffn_w[layer])
        last = layer == depth - 1
        next_w = final_norm_w if last else ln_mix_w[layer + 1]
        h, hn = conv_ffn(hn, ffn_w_up[layer].astype(BF16), ffn_conv_w[layer], ffn_conv_b[layer],
                         ffn_w_down[layer].astype(BF16), h, next_w, F32 if last else BF16)
        if layer == n_gdn - 1:
            kv = matmul_heads(rmsnorm(h, kv_norm_w, BF16), sb_w_kv.astype(BF16))
    return hn[None]
```

```python
import functools

import jax
import jax.numpy as jnp
from jax import lax
from jax.experimental import pallas as pl
from jax.experimental.pallas import tpu as pltpu

F32 = jnp.float32
BF16 = jnp.bfloat16
EPS = 1e-6

D_MODEL = 2048
N_HEADS = 16
HEAD_DIM = 128
D_FF = 5632
GDN_CHUNK = 64
GDN_BLOCK = 256
GDN_HEADS_PER_STEP = 4
SB_BLOCK = 256
SB_HEADS_PER_STEP = 2
SB_LOG_SURV_FLOOR = -104.0
VMEM_LIMIT = 56 * 1024 * 1024


HALO = 16


def _row_halo(x, tm):
    s, d = x.shape
    tails = x.reshape(s // tm, tm, d)[:, tm - HALO:, :]
    return jnp.concatenate([jnp.zeros((1, HALO, d), x.dtype), tails[:-1]], axis=0)


def _params(sem):
    return pltpu.CompilerParams(dimension_semantics=sem, vmem_limit_bytes=VMEM_LIMIT)


def _dot(a, b):
    return jnp.dot(a, b, preferred_element_type=F32)


def _dot_nt(a, b):
    return lax.dot_general(a, b, (((1,), (1,)), ((), ())), preferred_element_type=F32)


def _dot_tn(a, b):
    return lax.dot_general(a, b, (((0,), (0,)), ((), ())), preferred_element_type=F32)


def _split3(a):
    hi = a.astype(BF16)
    r1 = a - hi.astype(F32)
    mid = r1.astype(BF16)
    lo = (r1 - mid.astype(F32)).astype(BF16)
    return hi, mid, lo


def _sigmoid(x):
    return 0.5 + 0.5 * jnp.tanh(0.5 * x)


def _softplus(x):
    return jnp.maximum(x, 0.0) + jnp.log(1.0 + jnp.exp(-jnp.abs(x)))


def _rms(x, w):
    return x * lax.rsqrt(jnp.mean(x * x, axis=-1, keepdims=True) + EPS) * w


def _rmsnorm_kernel(x_ref, w_ref, o_ref):
    o_ref[...] = _rms(x_ref[...], w_ref[...]).astype(o_ref.dtype)


def rmsnorm(x, w, out_dtype, tm=512):
    s, d = x.shape
    return pl.pallas_call(
        _rmsnorm_kernel,
        grid=(s // tm,),
        in_specs=[pl.BlockSpec((tm, d), lambda i: (i, 0)), pl.BlockSpec((1, d), lambda i: (0, 0))],
        out_specs=pl.BlockSpec((tm, d), lambda i: (i, 0)),
        out_shape=jax.ShapeDtypeStruct((s, d), out_dtype),
        compiler_params=_params(("parallel",)),
        name="rmsnorm",
    )(x, w.reshape(1, d))


def _matmul_kernel(x_ref, w_ref, o_ref):
    o_ref[...] = _dot(x_ref[...], w_ref[...]).astype(o_ref.dtype)


def matmul(x, w, out_dtype, tm=1024, tn=1024):
    s, k = x.shape
    n = w.shape[1]
    return pl.pallas_call(
        _matmul_kernel,
        grid=(s // tm, n // tn),
        in_specs=[pl.BlockSpec((tm, k), lambda i, j: (i, 0)), pl.BlockSpec((k, tn), lambda i, j: (0, j))],
        out_specs=pl.BlockSpec((tm, tn), lambda i, j: (i, j)),
        out_shape=jax.ShapeDtypeStruct((s, n), out_dtype),
        compiler_params=_params(("parallel", "parallel")),
        name="matmul",
    )(x, w)


def _matmul_heads_kernel(x_ref, w_ref, o_ref, *, scale, heads_per_tile):
    acc = _dot(x_ref[...], w_ref[...])
    if scale is not None:
        acc = acc * scale
    for t in range(heads_per_tile):
        o_ref[t] = acc[:, t * HEAD_DIM:(t + 1) * HEAD_DIM].astype(o_ref.dtype)


def matmul_heads(x, w, scale=None, tm=1024, heads_per_tile=4):
    s, k = x.shape
    nh = w.shape[1] // HEAD_DIM
    tn = heads_per_tile * HEAD_DIM
    return pl.pallas_call(
        functools.partial(_matmul_heads_kernel, scale=scale, heads_per_tile=heads_per_tile),
        grid=(s // tm, nh // heads_per_tile),
        in_specs=[pl.BlockSpec((tm, k), lambda i, j: (i, 0)), pl.BlockSpec((k, tn), lambda i, j: (0, j))],
        out_specs=pl.BlockSpec((heads_per_tile, tm, HEAD_DIM), lambda i, j: (j, i, 0)),
        out_shape=jax.ShapeDtypeStruct((nh, s, HEAD_DIM), BF16),
        compiler_params=_params(("parallel", "parallel")),
        name="matmul_heads",
    )(x, w)


def _proj_resid_norm_kernel(a_ref, w_ref, r_ref, nw_ref, h_ref, hn_ref):
    h = r_ref[...] + _dot(a_ref[...], w_ref[...])
    h_ref[...] = h
    hn_ref[...] = _rms(h, nw_ref[...]).astype(hn_ref.dtype)


def proj_resid_norm(a, w, resid, norm_w, tm=256):
    s, k = a.shape
    d = w.shape[1]
    return pl.pallas_call(
        _proj_resid_norm_kernel,
        grid=(s // tm,),
        in_specs=[pl.BlockSpec((tm, k), lambda i: (i, 0)), pl.BlockSpec((k, d), lambda i: (0, 0)),
                  pl.BlockSpec((tm, d), lambda i: (i, 0)), pl.BlockSpec((1, d), lambda i: (0, 0))],
        out_specs=[pl.BlockSpec((tm, d), lambda i: (i, 0)), pl.BlockSpec((tm, d), lambda i: (i, 0))],
        out_shape=[jax.ShapeDtypeStruct((s, d), F32), jax.ShapeDtypeStruct((s, d), BF16)],
        compiler_params=_params(("parallel",)),
        name="proj_resid_norm",
    )(a, w, resid, norm_w.reshape(1, d))


def _chunk_masks(tb, chunk):
    shift = chunk.bit_length() - 1
    row = lax.broadcasted_iota(jnp.int32, (tb, tb), 0)
    col = lax.broadcasted_iota(jnp.int32, (tb, tb), 1)
    same = lax.shift_right_logical(row, shift) == lax.shift_right_logical(col, shift)
    return jnp.logical_and(same, row >= col), jnp.logical_and(same, row > col)


def _gdn_gates_kernel(hn_ref, wa_ref, wb_ref, alog_ref, dtb_ref, g_ref, gt_ref, beta_ref, *, tb, chunk):
    hn = hn_ref[...]
    a = _dot(hn, wa_ref[...])
    b = _dot(hn, wb_ref[...])
    g = -jnp.exp(alog_ref[...]) * _softplus(a + dtb_ref[...])
    causal, _ = _chunk_masks(tb, chunk)
    lower = causal.astype(BF16)
    g_hi, g_mid, g_lo = _split3(g)
    gcum = _dot(lower, g_hi) + _dot(lower, g_mid) + _dot(lower, g_lo)
    g_ref[...] = gcum
    gt_ref[...] = gcum.T[:N_HEADS, :]
    beta_ref[...] = _sigmoid(b)


def gdn_gates(hn, w_a, w_b, a_log, dt_bias, tb=GDN_BLOCK, chunk=GDN_CHUNK):
    s, d = hn.shape
    pad = lambda v: jnp.pad(v.astype(F32), (0, 128 - N_HEADS)).reshape(1, 128)
    return pl.pallas_call(
        functools.partial(_gdn_gates_kernel, tb=tb, chunk=chunk),
        grid=(s // tb,),
        in_specs=[pl.BlockSpec((tb, d), lambda i: (i, 0)),
                  pl.BlockSpec((d, 128), lambda i: (0, 0)), pl.BlockSpec((d, 128), lambda i: (0, 0)),
                  pl.BlockSpec((1, 128), lambda i: (0, 0)), pl.BlockSpec((1, 128), lambda i: (0, 0))],
        out_specs=[pl.BlockSpec((tb, 128), lambda i: (i, 0)),
                   pl.BlockSpec((N_HEADS, tb), lambda i: (0, i)),
                   pl.BlockSpec((tb, 128), lambda i: (i, 0))],
        out_shape=[jax.ShapeDtypeStruct((s, 128), F32), jax.ShapeDtypeStruct((N_HEADS, s), F32),
                   jax.ShapeDtypeStruct((s, 128), F32)],
        compiler_params=_params(("parallel",)),
        name="gdn_gates",
    )(hn, w_a, w_b, pad(a_log), pad(dt_bias))


def _mm_solve(a, b):
    return _dot(a.astype(BF16), b.astype(BF16))


def _gdn_core_kernel(q_ref, k_ref, v_ref, pg_ref, g_ref, gt_ref, beta_ref, onw_ref, o_ref, state, *, tb, chunk, hb):
    hg = pl.program_id(0)
    t = pl.program_id(1)

    @pl.when(t == 0)
    def _():
        state[...] = jnp.zeros_like(state)

    gblk = g_ref[...]
    bblk = beta_ref[...]
    lane = lax.broadcasted_iota(jnp.int32, (tb, 128), 1)
    causal, strict = _chunk_masks(tb, chunk)
    levels = chunk.bit_length() - 2

    heads = range(hb)
    sls = [slice(hh * HEAD_DIM, (hh + 1) * HEAD_DIM) for hh in heads]
    q, k, v, g_col, b_col, g_row, e_g = [], [], [], [], [], [], []
    for hh in heads:
        head = hg * hb + hh
        q.append(q_ref[:, sls[hh]])
        k.append(k_ref[:, sls[hh]])
        v.append(v_ref[:, sls[hh]])
        g_col.append(jnp.sum(jnp.where(lane == head, gblk, 0.0), axis=1, keepdims=True))
        b_col.append(jnp.sum(jnp.where(lane == head, bblk, 0.0), axis=1, keepdims=True))
        g_row.append(gt_ref[pl.ds(head, 1), :])
        e_g.append(jnp.exp(g_col[hh]))

    k_bf = [k[hh].astype(BF16) for hh in heads]
    k_beta = [k[hh] * b_col[hh] for hh in heads]
    decay = [jnp.where(causal, jnp.exp(jnp.minimum(g_col[hh] - g_row[hh], 0.0)), 0.0) for hh in heads]
    a_mat = [_dot_nt(k_beta[hh].astype(BF16), k_bf[hh]) for hh in heads]
    qp = [jnp.where(strict, -(a_mat[hh] * decay[hh]), 0.0) for hh in heads]
    rhs = [jnp.concatenate([v[hh] * b_col[hh], k_beta[hh] * e_g[hh]], axis=1) for hh in heads]
    sol = [rhs[hh] + _mm_solve(qp[hh], rhs[hh]) for hh in heads]
    for _ in range(levels):
        qp = [_mm_solve(qp[hh], qp[hh]) for hh in heads]
        sol = [sol[hh] + _mm_solve(qp[hh], sol[hh]) for hh in heads]
    u = [sol[hh][:, :HEAD_DIM] for hh in heads]
    w_bf = [sol[hh][:, HEAD_DIM:].astype(BF16) for hh in heads]
    attn = [(_dot_nt(q[hh].astype(BF16), k_bf[hh]) * decay[hh]).astype(BF16) for hh in heads]
    q_dec_bf = [(q[hh] * e_g[hh]).astype(BF16) for hh in heads]

    s = [state[hh] for hh in heads]
    v_new = [[] for _ in heads]
    o_state = [[] for _ in heads]
    for j in range(tb // chunk):
        r = slice(j * chunk, (j + 1) * chunk)
        for hh in heads:
            s_bf = s[hh].astype(BF16)
            v_new_j = (u[hh][r] - _dot(w_bf[hh][r], s_bf)).astype(BF16)
            o_state[hh].append(_dot(q_dec_bf[hh][r], s_bf))
            g_last = g_col[hh][(j + 1) * chunk - 1:(j + 1) * chunk, :]
            k_dec = k[hh][r] * jnp.exp(g_last - g_col[hh][r])
            s[hh] = s[hh] * jnp.exp(g_last) + _dot_tn(k_dec.astype(BF16), v_new_j)
            v_new[hh].append(v_new_j)
    for hh in heads:
        state[hh] = s[hh]
        o = jnp.concatenate(o_state[hh], axis=0) + _dot(attn[hh], jnp.concatenate(v_new[hh], axis=0))
        gate = pg_ref[:, sls[hh]]
        o_ref[:, sls[hh]] = (_rms(o, onw_ref[...]) * (gate * _sigmoid(gate))).astype(o_ref.dtype)


def gdn_core(qkv, gate, gcum, gcum_t, beta, o_norm_w, tb=GDN_BLOCK, chunk=GDN_CHUNK, hb=GDN_HEADS_PER_STEP):
    s = qkv.shape[0]
    w = hb * HEAD_DIM
    nhg = N_HEADS // hb
    pspec = lambda off: pl.BlockSpec((tb, w), lambda h, t, off=off: (t, off * nhg + h))
    return pl.pallas_call(
        functools.partial(_gdn_core_kernel, tb=tb, chunk=chunk, hb=hb),
        grid=(nhg, s // tb),
        in_specs=[pspec(0), pspec(1), pspec(2), pspec(0),
                  pl.BlockSpec((tb, 128), lambda h, t: (t, 0)),
                  pl.BlockSpec((N_HEADS, tb), lambda h, t: (0, t)),
                  pl.BlockSpec((tb, 128), lambda h, t: (t, 0)),
                  pl.BlockSpec((1, HEAD_DIM), lambda h, t: (0, 0))],
        out_specs=pl.BlockSpec((tb, w), lambda h, t: (t, h)),
        out_shape=jax.ShapeDtypeStruct((s, N_HEADS * HEAD_DIM), BF16),
        scratch_shapes=[pltpu.VMEM((hb, HEAD_DIM, HEAD_DIM), F32)],
        compiler_params=_params(("parallel", "arbitrary")),
        name="gdn_core",
    )(qkv, qkv, qkv, gate, gcum, gcum_t, beta, o_norm_w.reshape(1, HEAD_DIM))


def _in_proj_conv_kernel(x_ref, halo_ref, w_ref, cw_ref, o_ref, *u, tm, n_seg_tiles):
    j = pl.program_id(1)
    nsub = len(u)
    rows = tm // nsub
    w = w_ref[...]
    cw = cw_ref[...]
    is_qk = j < 2 * n_seg_tiles
    scale = jnp.where(j < n_seg_tiles, HEAD_DIM ** -0.5, 1.0).astype(F32)

    u[0][0:HALO, :] = _dot(halo_ref[0], w)
    for c in range(nsub):
        res = _dot(x_ref[c * rows:(c + 1) * rows, :], w)
        u[c][HALO:HALO + rows, :] = res
        if c + 1 < nsub:
            u[c + 1][0:HALO, :] = res[rows - HALO:, :]
    for c in range(nsub):
        uc = u[c][...]
        y = (cw[0:1, :] * pltpu.roll(uc, 3, 0)[HALO:, :] + cw[1:2, :] * pltpu.roll(uc, 2, 0)[HALO:, :]
             + cw[2:3, :] * pltpu.roll(uc, 1, 0)[HALO:, :] + cw[3:4, :] * uc[HALO:, :])
        y = y * _sigmoid(y)
        for hd in range(y.shape[1] // HEAD_DIM):
            sl = slice(hd * HEAD_DIM, (hd + 1) * HEAD_DIM)
            yh = y[:, sl]
            inv = lax.rsqrt(jnp.sum(yh * yh, axis=-1, keepdims=True) + EPS) * scale
            o_ref[c * rows:(c + 1) * rows, sl] = yh * jnp.where(is_qk, inv, 1.0)


def in_proj_conv(x, w, conv_w, tm=1024, tn=1024, nsub=4):
    s, k = x.shape
    n = w.shape[1]
    n_seg_tiles = (n // 3) // tn
    return pl.pallas_call(
        functools.partial(_in_proj_conv_kernel, tm=tm, n_seg_tiles=n_seg_tiles),
        grid=(s // tm, n // tn),
        in_specs=[pl.BlockSpec((tm, k), lambda i, j: (i, 0)),
                  pl.BlockSpec((1, HALO, k), lambda i, j: (i, 0, 0)),
                  pl.BlockSpec((k, tn), lambda i, j: (0, j)),
                  pl.BlockSpec((4, tn), lambda i, j: (0, j))],
        out_specs=pl.BlockSpec((tm, tn), lambda i, j: (i, j)),
        out_shape=jax.ShapeDtypeStruct((s, n), F32),
        scratch_shapes=[pltpu.VMEM((tm // nsub + HALO, tn), F32) for _ in range(nsub)],
        compiler_params=_params(("parallel", "parallel")),
        name="in_proj_conv",
    )(x, _row_halo(x, tm), w, conv_w)


def _sb_kernel(q_ref, k_ref, v_ref, o_ref, acc, ls, *, blk, hb):
    i = pl.program_id(1)
    heads = range(hb)
    row = lax.broadcasted_iota(jnp.int32, (blk, blk), 0)
    col = lax.broadcasted_iota(jnp.int32, (blk, blk), 1)
    suffix = (row >= col).astype(BF16)
    valid = col < row
    acc[...] = jnp.zeros_like(acc)
    ls[...] = jnp.zeros_like(ls)

    def sweep(kb, diagonal):
        off = pl.multiple_of(kb * blk, blk)
        z = [_dot_nt(q_ref[hh], k_ref[hh, pl.ds(off, blk), :]) for hh in heads]
        log_1m = [-_softplus(z[hh]) for hh in heads]
        if diagonal:
            log_1m = [jnp.where(valid, log_1m[hh], 0.0) for hh in heads]
        parts = [_split3(log_1m[hh]) for hh in heads]
        incl = [_dot(parts[hh][0], suffix) + _dot(parts[hh][1], suffix) + _dot(parts[hh][2], suffix) for hh in heads]
        p = [jnp.exp(z[hh] + incl[hh] + ls[hh]) for hh in heads]
        if diagonal:
            p = [jnp.where(valid, p[hh], 0.0) for hh in heads]
        m = None
        for hh in heads:
            acc[hh] += _dot(p[hh].astype(BF16), v_ref[hh, pl.ds(off, blk), :])
            ls_new = ls[hh] + incl[hh][:, 0:1]
            ls[hh] = ls_new
            mh = jnp.max(ls_new)
            m = mh if m is None else jnp.maximum(m, mh)
        return m

    m0 = sweep(i, True)

    def cond(carry):
        kb, m = carry
        return jnp.logical_and(kb >= 0, m > SB_LOG_SURV_FLOOR)

    def body(carry):
        kb, _ = carry
        return kb - 1, sweep(kb, False)

    lax.while_loop(cond, body, (i - 1, m0))
    for hh in heads:
        o_ref[:, hh * HEAD_DIM:(hh + 1) * HEAD_DIM] = acc[hh].astype(o_ref.dtype)


def sb_attention(q, kv, blk=SB_BLOCK, hb=SB_HEADS_PER_STEP):
    nh, s, d = q.shape
    nhg = nh // hb
    return pl.pallas_call(
        functools.partial(_sb_kernel, blk=blk, hb=hb),
        grid=(nhg, s // blk),
        in_specs=[pl.BlockSpec((hb, blk, d), lambda h, i: (h, i, 0)),
                  pl.BlockSpec((hb, s, d), lambda h, i: (h, 0, 0)),
                  pl.BlockSpec((hb, s, d), lambda h, i: (nhg + h, 0, 0))],
        out_specs=pl.BlockSpec((blk, hb * d), lambda h, i: (i, h)),
        out_shape=jax.ShapeDtypeStruct((s, nh * d), BF16),
        scratch_shapes=[pltpu.VMEM((hb, blk, d), F32), pltpu.VMEM((hb, blk, 1), F32)],
        compiler_params=_params(("parallel", "arbitrary")),
        name="sb_attention",
    )(q, kv, kv)


def _ffn_kernel(x_ref, halo_ref, wg_ref, wv_ref, cwg_ref, cwv_ref, cbg_ref, cbv_ref, wd_ref, r_ref, nw_ref,
                h_ref, hn_ref, *scratch, tm, nj, nsub):
    j = pl.program_id(1)
    rows = tm // nsub
    wg = wg_ref[...]
    wv = wv_ref[...]

    @pl.when(j == 0)
    def _():
        h_ref[...] = r_ref[...]

    scratch[0][0:HALO, :] = _dot(halo_ref[0], wg)
    scratch[1][0:HALO, :] = _dot(halo_ref[0], wv)
    for c in range(nsub):
        xc = x_ref[c * rows:(c + 1) * rows, :]
        for br, w in enumerate((wg, wv)):
            res = _dot(xc, w)
            scratch[2 * c + br][HALO:HALO + rows, :] = res
            if c + 1 < nsub:
                scratch[2 * (c + 1) + br][0:HALO, :] = res[rows - HALO:, :]

    def conv(u, cw_ref, cb_ref):
        cw = cw_ref[...]
        return (cw[0:1, :] * u[HALO - 2:HALO - 2 + rows, :] + cw[1:2, :] * u[HALO - 1:HALO - 1 + rows, :]
                + cw[2:3, :] * u[HALO:HALO + rows, :] + cb_ref[...])

    wd = wd_ref[...]
    for c in range(nsub):
        yg = conv(scratch[2 * c], cwg_ref, cbg_ref)
        yv = conv(scratch[2 * c + 1], cwv_ref, cbv_ref)
        act = ((yg * _sigmoid(yg)) * yv).astype(BF16)
        h_ref[c * rows:(c + 1) * rows, :] += _dot(act, wd)

    @pl.when(j == nj - 1)
    def _():
        hn_ref[...] = _rms(h_ref[...], nw_ref[...]).astype(hn_ref.dtype)


def conv_ffn(hn, w_up, conv_w, conv_b, w_down, resid, norm_w, norm_dtype, tm=512, tf=512, nsub=2):
    s, d = hn.shape
    nt = s // tm
    nj = D_FF // tf
    conv_b = conv_b.reshape(1, 2 * D_FF)
    return pl.pallas_call(
        functools.partial(_ffn_kernel, tm=tm, nj=nj, nsub=nsub),
        grid=(nt, nj),
        in_specs=[pl.BlockSpec((tm, d), lambda i, j: (i, 0)),
                  pl.BlockSpec((1, HALO, d), lambda i, j: (i, 0, 0)),
                  pl.BlockSpec((d, tf), lambda i, j: (0, j)),
                  pl.BlockSpec((d, tf), lambda i, j: (0, nj + j)),
                  pl.BlockSpec((3, tf), lambda i, j: (0, j)),
                  pl.BlockSpec((3, tf), lambda i, j: (0, nj + j)),
                  pl.BlockSpec((1, tf), lambda i, j: (0, j)),
                  pl.BlockSpec((1, tf), lambda i, j: (0, nj + j)),
                  pl.BlockSpec((tf, d), lambda i, j: (j, 0)),
                  pl.BlockSpec((tm, d), lambda i, j: (i, 0)),
                  pl.BlockSpec((1, d), lambda i, j: (0, 0))],
        out_specs=[pl.BlockSpec((tm, d), lambda i, j: (i, 0)), pl.BlockSpec((tm, d), lambda i, j: (i, 0))],
        out_shape=[jax.ShapeDtypeStruct((s, d), F32), jax.ShapeDtypeStruct((s, d), norm_dtype)],
        scratch_shapes=[pltpu.VMEM((tm // nsub + HALO, tf), F32) for _ in range(2 * nsub)],
        compiler_params=_params(("parallel", "arbitrary")),
        name="conv_ffn",
    )(hn, _row_halo(hn, tm), w_up, w_up, conv_w, conv_w, conv_b, conv_b, w_down, resid, norm_w.reshape(1, d))


def kernel(x, ln_mix_w, ln_ffn_w, gdn_w_in, gdn_conv_w, gdn_a_log, gdn_dt_bias, gdn_o_norm_w, gdn_w_out,
           kv_norm_w, sb_w_kv, sb_w_q, sb_w_out, ffn_w_up, ffn_conv_w, ffn_conv_b, ffn_w_down, final_norm_w):
    depth = ln_mix_w.shape[0]
    n_gdn = gdn_w_in.shape[0]
    qkv_dim = 3 * N_HEADS * HEAD_DIM
    qkvg = 4 * N_HEADS * HEAD_DIM
    h = x[0]
    hn = rmsnorm(h, ln_mix_w[0], BF16)
    kv = None
    for layer in range(depth):
        if layer < n_gdn:
            w_in = gdn_w_in[layer]
            pad_w = lambda w: jnp.pad(w, ((0, 0), (0, 128 - N_HEADS))).astype(BF16)
            qkv = in_proj_conv(hn, w_in[:, :qkv_dim].astype(BF16), gdn_conv_w[layer])
            gate = matmul(hn, w_in[:, qkv_dim:qkvg].astype(BF16), F32)
            gcum, gcum_t, beta = gdn_gates(hn, pad_w(w_in[:, qkvg:qkvg + N_HEADS]), pad_w(w_in[:, qkvg + N_HEADS:]),
                                           gdn_a_log[layer], gdn_dt_bias[layer])
            o = gdn_core(qkv, gate, gcum, gcum_t, beta, gdn_o_norm_w[layer])
            h, hn = proj_resid_norm(o, gdn_w_out[layer].astype(BF16), h, ln_ffn_w[layer])
        else:
            j = layer - n_gdn
            q = matmul_heads(hn, sb_w_q[j].astype(BF16), scale=HEAD_DIM ** -0.5)
            o = sb_attention(q, kv)
            h, hn = proj_resid_norm(o, sb_w_out[j].astype(BF16), h, ln_ffn_w[layer])
        last = layer == depth - 1
        next_w = final_norm_w if last else ln_mix_w[layer + 1]
        h, hn = conv_ffn(hn, ffn_w_up[layer].astype(BF16), ffn_conv_w[layer], ffn_conv_b[layer],
                         ffn_w_down[layer].astype(BF16), h, next_w, F32 if last else BF16)
        if layer == n_gdn - 1:
            kv = matmul_heads(rmsnorm(h, kv_norm_w, BF16), sb_w_kv.astype(BF16))
    return hn[None]
```

```python
import functools

import jax
import jax.numpy as jnp
from jax import lax
from jax.experimental import pallas as pl
from jax.experimental.pallas import tpu as pltpu

F32 = jnp.float32
BF16 = jnp.bfloat16
EPS = 1e-6

D_MODEL = 2048
N_HEADS = 16
HEAD_DIM = 128
D_FF = 5632
GDN_CHUNK = 64
GDN_BLOCK = 256
GDN_HEADS_PER_STEP = 4
SB_BLOCK = 256
SB_HEADS_PER_STEP = 4
SB_LOG_SURV_FLOOR = -104.0
VMEM_LIMIT = 56 * 1024 * 1024


HALO = 16


def _row_halo(x, tm):
    s, d = x.shape
    tails = x.reshape(s // tm, tm, d)[:, tm - HALO:, :]
    return jnp.concatenate([jnp.zeros((1, HALO, d), x.dtype), tails[:-1]], axis=0)


def _params(sem):
    return pltpu.CompilerParams(dimension_semantics=sem, vmem_limit_bytes=VMEM_LIMIT)


def _dot(a, b):
    return jnp.dot(a, b, preferred_element_type=F32)


def _dot_nt(a, b):
    return lax.dot_general(a, b, (((1,), (1,)), ((), ())), preferred_element_type=F32)


def _dot_tn(a, b):
    return lax.dot_general(a, b, (((0,), (0,)), ((), ())), preferred_element_type=F32)


def _split3(a):
    hi = a.astype(BF16)
    r1 = a - hi.astype(F32)
    mid = r1.astype(BF16)
    lo = (r1 - mid.astype(F32)).astype(BF16)
    return hi, mid, lo


def _sigmoid(x):
    return 0.5 + 0.5 * jnp.tanh(0.5 * x)


def _silu(x):
    h = 0.5 * x
    return h + h * jnp.tanh(h)


def _softplus(x):
    return jnp.maximum(x, 0.0) + jnp.log(1.0 + jnp.exp(-jnp.abs(x)))


def _rms(x, w):
    return x * lax.rsqrt(jnp.mean(x * x, axis=-1, keepdims=True) + EPS) * w


def _rmsnorm_kernel(x_ref, w_ref, o_ref):
    o_ref[...] = _rms(x_ref[...], w_ref[...]).astype(o_ref.dtype)


def rmsnorm(x, w, out_dtype, tm=512):
    s, d = x.shape
    return pl.pallas_call(
        _rmsnorm_kernel,
        grid=(s // tm,),
        in_specs=[pl.BlockSpec((tm, d), lambda i: (i, 0)), pl.BlockSpec((1, d), lambda i: (0, 0))],
        out_specs=pl.BlockSpec((tm, d), lambda i: (i, 0)),
        out_shape=jax.ShapeDtypeStruct((s, d), out_dtype),
        compiler_params=_params(("parallel",)),
        name="rmsnorm",
    )(x, w.reshape(1, d))


def _matmul_kernel(x_ref, w_ref, o_ref):
    o_ref[...] = _dot(x_ref[...], w_ref[...]).astype(o_ref.dtype)


def matmul(x, w, out_dtype, tm=1024, tn=1024):
    s, k = x.shape
    n = w.shape[1]
    return pl.pallas_call(
        _matmul_kernel,
        grid=(s // tm, n // tn),
        in_specs=[pl.BlockSpec((tm, k), lambda i, j: (i, 0)), pl.BlockSpec((k, tn), lambda i, j: (0, j))],
        out_specs=pl.BlockSpec((tm, tn), lambda i, j: (i, j)),
        out_shape=jax.ShapeDtypeStruct((s, n), out_dtype),
        compiler_params=_params(("parallel", "parallel")),
        name="matmul",
    )(x, w)


def _matmul_heads_kernel(x_ref, w_ref, o_ref, *, scale, heads_per_tile):
    acc = _dot(x_ref[...], w_ref[...])
    if scale is not None:
        acc = acc * scale
    for t in range(heads_per_tile):
        o_ref[t] = acc[:, t * HEAD_DIM:(t + 1) * HEAD_DIM].astype(o_ref.dtype)


def matmul_heads(x, w, scale=None, tm=1024, heads_per_tile=4):
    s, k = x.shape
    nh = w.shape[1] // HEAD_DIM
    tn = heads_per_tile * HEAD_DIM
    return pl.pallas_call(
        functools.partial(_matmul_heads_kernel, scale=scale, heads_per_tile=heads_per_tile),
        grid=(s // tm, nh // heads_per_tile),
        in_specs=[pl.BlockSpec((tm, k), lambda i, j: (i, 0)), pl.BlockSpec((k, tn), lambda i, j: (0, j))],
        out_specs=pl.BlockSpec((heads_per_tile, tm, HEAD_DIM), lambda i, j: (j, i, 0)),
        out_shape=jax.ShapeDtypeStruct((nh, s, HEAD_DIM), BF16),
        compiler_params=_params(("parallel", "parallel")),
        name="matmul_heads",
    )(x, w)


def _proj_resid_norm_kernel(a_ref, w_ref, r_ref, nw_ref, h_ref, hn_ref):
    h = r_ref[...] + _dot(a_ref[...], w_ref[...])
    h_ref[...] = h
    hn_ref[...] = _rms(h, nw_ref[...]).astype(hn_ref.dtype)


def proj_resid_norm(a, w, resid, norm_w, tm=256):
    s, k = a.shape
    d = w.shape[1]
    return pl.pallas_call(
        _proj_resid_norm_kernel,
        grid=(s // tm,),
        in_specs=[pl.BlockSpec((tm, k), lambda i: (i, 0)), pl.BlockSpec((k, d), lambda i: (0, 0)),
                  pl.BlockSpec((tm, d), lambda i: (i, 0)), pl.BlockSpec((1, d), lambda i: (0, 0))],
        out_specs=[pl.BlockSpec((tm, d), lambda i: (i, 0)), pl.BlockSpec((tm, d), lambda i: (i, 0))],
        out_shape=[jax.ShapeDtypeStruct((s, d), F32), jax.ShapeDtypeStruct((s, d), BF16)],
        compiler_params=_params(("parallel",)),
        name="proj_resid_norm",
    )(a, w, resid, norm_w.reshape(1, d))


def _chunk_masks(tb, chunk):
    shift = chunk.bit_length() - 1
    row = lax.broadcasted_iota(jnp.int32, (tb, tb), 0)
    col = lax.broadcasted_iota(jnp.int32, (tb, tb), 1)
    same = lax.shift_right_logical(row, shift) == lax.shift_right_logical(col, shift)
    return jnp.logical_and(same, row >= col), jnp.logical_and(same, row > col)


def _gdn_gates_kernel(hn_ref, wa_ref, wb_ref, alog_ref, dtb_ref, g_ref, gt_ref, beta_ref, *, tb, chunk):
    hn = hn_ref[...]
    a = _dot(hn, wa_ref[...])
    b = _dot(hn, wb_ref[...])
    g = -jnp.exp(alog_ref[...]) * _softplus(a + dtb_ref[...])
    causal, _ = _chunk_masks(tb, chunk)
    lower = causal.astype(BF16)
    g_hi, g_mid, g_lo = _split3(g)
    gcum = _dot(lower, g_hi) + _dot(lower, g_mid) + _dot(lower, g_lo)
    g_ref[...] = gcum
    gt_ref[...] = gcum.T[:N_HEADS, :]
    beta_ref[...] = _sigmoid(b)


def gdn_gates(hn, w_a, w_b, a_log, dt_bias, tb=GDN_BLOCK, chunk=GDN_CHUNK):
    s, d = hn.shape
    pad = lambda v: jnp.pad(v.astype(F32), (0, 128 - N_HEADS)).reshape(1, 128)
    return pl.pallas_call(
        functools.partial(_gdn_gates_kernel, tb=tb, chunk=chunk),
        grid=(s // tb,),
        in_specs=[pl.BlockSpec((tb, d), lambda i: (i, 0)),
                  pl.BlockSpec((d, 128), lambda i: (0, 0)), pl.BlockSpec((d, 128), lambda i: (0, 0)),
                  pl.BlockSpec((1, 128), lambda i: (0, 0)), pl.BlockSpec((1, 128), lambda i: (0, 0))],
        out_specs=[pl.BlockSpec((tb, 128), lambda i: (i, 0)),
                   pl.BlockSpec((N_HEADS, tb), lambda i: (0, i)),
                   pl.BlockSpec((tb, 128), lambda i: (i, 0))],
        out_shape=[jax.ShapeDtypeStruct((s, 128), F32), jax.ShapeDtypeStruct((N_HEADS, s), F32),
                   jax.ShapeDtypeStruct((s, 128), F32)],
        compiler_params=_params(("parallel",)),
        name="gdn_gates",
    )(hn, w_a, w_b, pad(a_log), pad(dt_bias))


def _mm_solve(a, b):
    return _dot(a.astype(BF16), b.astype(BF16))


def _gdn_core_kernel(q_ref, k_ref, v_ref, pg_ref, g_ref, gt_ref, beta_ref, onw_ref, o_ref, state, *, tb, chunk, hb):
    hg = pl.program_id(0)
    t = pl.program_id(1)

    @pl.when(t == 0)
    def _():
        state[...] = jnp.zeros_like(state)

    gblk = g_ref[...]
    bblk = beta_ref[...]
    lane = lax.broadcasted_iota(jnp.int32, (tb, 128), 1)
    causal, strict = _chunk_masks(tb, chunk)
    levels = chunk.bit_length() - 2

    heads = range(hb)
    sls = [slice(hh * HEAD_DIM, (hh + 1) * HEAD_DIM) for hh in heads]
    q, k, v, g_col, b_col, g_row, e_g = [], [], [], [], [], [], []
    for hh in heads:
        head = hg * hb + hh
        q.append(q_ref[:, sls[hh]])
        k.append(k_ref[:, sls[hh]])
        v.append(v_ref[:, sls[hh]])
        g_col.append(jnp.sum(jnp.where(lane == head, gblk, 0.0), axis=1, keepdims=True))
        b_col.append(jnp.sum(jnp.where(lane == head, bblk, 0.0), axis=1, keepdims=True))
        g_row.append(gt_ref[pl.ds(head, 1), :])
        e_g.append(jnp.exp(g_col[hh]))

    k_bf = [k[hh].astype(BF16) for hh in heads]
    k_beta = [k[hh] * b_col[hh] for hh in heads]
    decay = [jnp.where(causal, jnp.exp(jnp.minimum(g_col[hh] - g_row[hh], 0.0)), 0.0) for hh in heads]
    a_mat = [_dot_nt(k_beta[hh].astype(BF16), k_bf[hh]) for hh in heads]
    qp = [jnp.where(strict, -(a_mat[hh] * decay[hh]), 0.0) for hh in heads]
    rhs = [jnp.concatenate([v[hh] * b_col[hh], k_beta[hh] * e_g[hh]], axis=1) for hh in heads]
    sol = [rhs[hh] + _mm_solve(qp[hh], rhs[hh]) for hh in heads]
    for _ in range(levels):
        qp = [_mm_solve(qp[hh], qp[hh]) for hh in heads]
        sol = [sol[hh] + _mm_solve(qp[hh], sol[hh]) for hh in heads]
    u = [sol[hh][:, :HEAD_DIM] for hh in heads]
    w_bf = [sol[hh][:, HEAD_DIM:].astype(BF16) for hh in heads]
    attn = [(_dot_nt(q[hh].astype(BF16), k_bf[hh]) * decay[hh]).astype(BF16) for hh in heads]
    q_dec_bf = [(q[hh] * e_g[hh]).astype(BF16) for hh in heads]

    s = [state[hh] for hh in heads]
    v_new = [[] for _ in heads]
    o_state = [[] for _ in heads]
    for j in range(tb // chunk):
        r = slice(j * chunk, (j + 1) * chunk)
        for hh in heads:
            s_bf = s[hh].astype(BF16)
            v_new_j = (u[hh][r] - _dot(w_bf[hh][r], s_bf)).astype(BF16)
            o_state[hh].append(_dot(q_dec_bf[hh][r], s_bf))
            g_last = g_col[hh][(j + 1) * chunk - 1:(j + 1) * chunk, :]
            k_dec = k[hh][r] * jnp.exp(g_last - g_col[hh][r])
            s[hh] = s[hh] * jnp.exp(g_last) + _dot_tn(k_dec.astype(BF16), v_new_j)
            v_new[hh].append(v_new_j)
    for hh in heads:
        state[hh] = s[hh]
        o = jnp.concatenate(o_state[hh], axis=0) + _dot(attn[hh], jnp.concatenate(v_new[hh], axis=0))
        gate = pg_ref[:, sls[hh]]
        o_ref[:, sls[hh]] = (_rms(o, onw_ref[...]) * _silu(gate)).astype(o_ref.dtype)


def gdn_core(qkv, gate, gcum, gcum_t, beta, o_norm_w, tb=GDN_BLOCK, chunk=GDN_CHUNK, hb=GDN_HEADS_PER_STEP):
    s = qkv.shape[0]
    w = hb * HEAD_DIM
    nhg = N_HEADS // hb
    pspec = lambda off: pl.BlockSpec((tb, w), lambda h, t, off=off: (t, off * nhg + h))
    return pl.pallas_call(
        functools.partial(_gdn_core_kernel, tb=tb, chunk=chunk, hb=hb),
        grid=(nhg, s // tb),
        in_specs=[pspec(0), pspec(1), pspec(2), pspec(0),
                  pl.BlockSpec((tb, 128), lambda h, t: (t, 0)),
                  pl.BlockSpec((N_HEADS, tb), lambda h, t: (0, t)),
                  pl.BlockSpec((tb, 128), lambda h, t: (t, 0)),
                  pl.BlockSpec((1, HEAD_DIM), lambda h, t: (0, 0))],
        out_specs=pl.BlockSpec((tb, w), lambda h, t: (t, h)),
        out_shape=jax.ShapeDtypeStruct((s, N_HEADS * HEAD_DIM), BF16),
        scratch_shapes=[pltpu.VMEM((hb, HEAD_DIM, HEAD_DIM), F32)],
        compiler_params=_params(("parallel", "arbitrary")),
        name="gdn_core",
    )(qkv, qkv, qkv, gate, gcum, gcum_t, beta, o_norm_w.reshape(1, HEAD_DIM))


def _in_proj_conv_kernel(x_ref, halo_ref, w_ref, cw_ref, o_ref, *u, tm, n_seg_tiles):
    j = pl.program_id(1)
    x_ext = u[-1]
    u = u[:-1]
    nsub = len(u)
    rows = tm // nsub
    w = w_ref[...]
    cw = cw_ref[...]
    is_qk = j < 2 * n_seg_tiles
    scale = jnp.where(j < n_seg_tiles, HEAD_DIM ** -0.5, 1.0).astype(F32)

    @pl.when(j == 0)
    def _():
        x_ext[0:HALO, :] = halo_ref[0]
        x_ext[HALO:HALO + tm, :] = x_ref[...]

    for c in range(nsub):
        u[c][...] = _dot(x_ext[c * rows:c * rows + rows + HALO, :], w)
    for c in range(nsub):
        uc = u[c][...]
        y = (cw[0:1, :] * pltpu.roll(uc, 3, 0)[HALO:, :] + cw[1:2, :] * pltpu.roll(uc, 2, 0)[HALO:, :]
             + cw[2:3, :] * pltpu.roll(uc, 1, 0)[HALO:, :] + cw[3:4, :] * uc[HALO:, :])
        y = _silu(y)
        for hd in range(y.shape[1] // HEAD_DIM):
            sl = slice(hd * HEAD_DIM, (hd + 1) * HEAD_DIM)
            yh = y[:, sl]
            inv = lax.rsqrt(jnp.sum(yh * yh, axis=-1, keepdims=True) + EPS) * scale
            o_ref[c * rows:(c + 1) * rows, sl] = yh * jnp.where(is_qk, inv, 1.0)


def in_proj_conv(x, w, conv_w, tm=1024, tn=1024, nsub=4):
    s, k = x.shape
    n = w.shape[1]
    n_seg_tiles = (n // 3) // tn
    return pl.pallas_call(
        functools.partial(_in_proj_conv_kernel, tm=tm, n_seg_tiles=n_seg_tiles),
        grid=(s // tm, n // tn),
        in_specs=[pl.BlockSpec((tm, k), lambda i, j: (i, 0)),
                  pl.BlockSpec((1, HALO, k), lambda i, j: (i, 0, 0)),
                  pl.BlockSpec((k, tn), lambda i, j: (0, j)),
                  pl.BlockSpec((4, tn), lambda i, j: (0, j))],
        out_specs=pl.BlockSpec((tm, tn), lambda i, j: (i, j)),
        out_shape=jax.ShapeDtypeStruct((s, n), F32),
        scratch_shapes=([pltpu.VMEM((tm // nsub + HALO, tn), F32) for _ in range(nsub)]
                        + [pltpu.VMEM((tm + HALO, k), x.dtype)]),
        compiler_params=_params(("parallel", "arbitrary")),
        name="in_proj_conv",
    )(x, _row_halo(x, tm), w, conv_w)


def _sb_kernel(q_ref, k_ref, v_ref, o_ref, acc, ls, *, blk, hb):
    i = pl.program_id(1)
    heads = range(hb)
    row = lax.broadcasted_iota(jnp.int32, (blk, blk), 0)
    col = lax.broadcasted_iota(jnp.int32, (blk, blk), 1)
    suffix = (row >= col).astype(BF16)
    valid = col < row
    acc[...] = jnp.zeros_like(acc)
    ls[...] = jnp.zeros_like(ls)

    def sweep(kb, diagonal):
        off = pl.multiple_of(kb * blk, blk)
        z = [_dot_nt(q_ref[hh], k_ref[hh, pl.ds(off, blk), :]) for hh in heads]
        log_1m = [-_softplus(z[hh]) for hh in heads]
        if diagonal:
            log_1m = [jnp.where(valid, log_1m[hh], 0.0) for hh in heads]
        parts = [_split3(log_1m[hh]) for hh in heads]
        incl = [_dot(parts[hh][0], suffix) + _dot(parts[hh][1], suffix) + _dot(parts[hh][2], suffix) for hh in heads]
        p = [jnp.exp(z[hh] + incl[hh] + ls[hh]) for hh in heads]
        if diagonal:
            p = [jnp.where(valid, p[hh], 0.0) for hh in heads]
        m = None
        for hh in heads:
            acc[hh] += _dot(p[hh].astype(BF16), v_ref[hh, pl.ds(off, blk), :])
            ls_new = ls[hh] + incl[hh][:, 0:1]
            ls[hh] = ls_new
            mh = jnp.max(ls_new)
            m = mh if m is None else jnp.maximum(m, mh)
        return m

    m0 = sweep(i, True)

    def cond(carry):
        kb, m = carry
        return jnp.logical_and(kb >= 0, m > SB_LOG_SURV_FLOOR)

    def body(carry):
        kb, _ = carry
        return kb - 1, sweep(kb, False)

    lax.while_loop(cond, body, (i - 1, m0))
    for hh in heads:
        o_ref[:, hh * HEAD_DIM:(hh + 1) * HEAD_DIM] = acc[hh].astype(o_ref.dtype)


def sb_attention(q, kv, blk=SB_BLOCK, hb=SB_HEADS_PER_STEP):
    nh, s, d = q.shape
    nhg = nh // hb
    return pl.pallas_call(
        functools.partial(_sb_kernel, blk=blk, hb=hb),
        grid=(nhg, s // blk),
        in_specs=[pl.BlockSpec((hb, blk, d), lambda h, i: (h, i, 0)),
                  pl.BlockSpec((hb, s, d), lambda h, i: (h, 0, 0), pipeline_mode=pl.Buffered(1)),
                  pl.BlockSpec((hb, s, d), lambda h, i: (nhg + h, 0, 0), pipeline_mode=pl.Buffered(1))],
        out_specs=pl.BlockSpec((blk, hb * d), lambda h, i: (i, h)),
        out_shape=jax.ShapeDtypeStruct((s, nh * d), BF16),
        scratch_shapes=[pltpu.VMEM((hb, blk, d), F32), pltpu.VMEM((hb, blk, 1), F32)],
        compiler_params=_params(("parallel", "arbitrary")),
        name="sb_attention",
    )(q, kv, kv)


def _ffn_kernel(x_ref, halo_ref, wg_ref, wv_ref, cwg_ref, cwv_ref, cbg_ref, cbv_ref, wd_ref, r_ref, nw_ref,
                h_ref, hn_ref, *scratch, tm, nj, nsub):
    j = pl.program_id(1)
    x_ext = scratch[-1]
    u = scratch[:-1]
    rows = tm // nsub
    wg = wg_ref[...]
    wv = wv_ref[...]

    @pl.when(j == 0)
    def _():
        h_ref[...] = r_ref[...]
        x_ext[0:HALO, :] = halo_ref[0]
        x_ext[HALO:HALO + tm, :] = x_ref[...]

    for c in range(nsub):
        xc = x_ext[c * rows:c * rows + rows + HALO, :]
        u[2 * c][...] = _dot(xc, wg)
        u[2 * c + 1][...] = _dot(xc, wv)

    def conv(uc, cw_ref, cb_ref):
        cw = cw_ref[...]
        return (cw[0:1, :] * uc[HALO - 2:HALO - 2 + rows, :] + cw[1:2, :] * uc[HALO - 1:HALO - 1 + rows, :]
                + cw[2:3, :] * uc[HALO:HALO + rows, :] + cb_ref[...])

    wd = wd_ref[...]
    for c in range(nsub):
        yg = conv(u[2 * c], cwg_ref, cbg_ref)
        yv = conv(u[2 * c + 1], cwv_ref, cbv_ref)
        act = (_silu(yg) * yv).astype(BF16)
        h_ref[c * rows:(c + 1) * rows, :] += _dot(act, wd)

    @pl.when(j == nj - 1)
    def _():
        hn_ref[...] = _rms(h_ref[...], nw_ref[...]).astype(hn_ref.dtype)


def conv_ffn(hn, w_up, conv_w, conv_b, w_down, resid, norm_w, norm_dtype, tm=512, tf=512, nsub=2):
    s, d = hn.shape
    nt = s // tm
    nj = D_FF // tf
    conv_b = conv_b.reshape(1, 2 * D_FF)
    return pl.pallas_call(
        functools.partial(_ffn_kernel, tm=tm, nj=nj, nsub=nsub),
        grid=(nt, nj),
        in_specs=[pl.BlockSpec((tm, d), lambda i, j: (i, 0)),
                  pl.BlockSpec((1, HALO, d), lambda i, j: (i, 0, 0)),
                  pl.BlockSpec((d, tf), lambda i, j: (0, j)),
                  pl.BlockSpec((d, tf), lambda i, j: (0, nj + j)),
                  pl.BlockSpec((3, tf), lambda i, j: (0, j)),
                  pl.BlockSpec((3, tf), lambda i, j: (0, nj + j)),
                  pl.BlockSpec((1, tf), lambda i, j: (0, j)),
                  pl.BlockSpec((1, tf), lambda i, j: (0, nj + j)),
                  pl.BlockSpec((tf, d), lambda i, j: (j, 0)),
                  pl.BlockSpec((tm, d), lambda i, j: (i, 0)),
                  pl.BlockSpec((1, d), lambda i, j: (0, 0))],
        out_specs=[pl.BlockSpec((tm, d), lambda i, j: (i, 0)), pl.BlockSpec((tm, d), lambda i, j: (i, 0))],
        out_shape=[jax.ShapeDtypeStruct((s, d), F32), jax.ShapeDtypeStruct((s, d), norm_dtype)],
        scratch_shapes=([pltpu.VMEM((tm // nsub + HALO, tf), F32) for _ in range(2 * nsub)]
                        + [pltpu.VMEM((tm + HALO, d), hn.dtype)]),
        compiler_params=_params(("parallel", "arbitrary")),
        name="conv_ffn",
    )(hn, _row_halo(hn, tm), w_up, w_up, conv_w, conv_w, conv_b, conv_b, w_down, resid, norm_w.reshape(1, d))


def kernel(x, ln_mix_w, ln_ffn_w, gdn_w_in, gdn_conv_w, gdn_a_log, gdn_dt_bias, gdn_o_norm_w, gdn_w_out,
           kv_norm_w, sb_w_kv, sb_w_q, sb_w_out, ffn_w_up, ffn_conv_w, ffn_conv_b, ffn_w_down, final_norm_w):
    depth = ln_mix_w.shape[0]
    n_gdn = gdn_w_in.shape[0]
    qkv_dim = 3 * N_HEADS * HEAD_DIM
    qkvg = 4 * N_HEADS * HEAD_DIM
    h = x[0]
    hn = rmsnorm(h, ln_mix_w[0], BF16)
    kv = None
    for layer in range(depth):
        if layer < n_gdn:
            w_in = gdn_w_in[layer]
            pad_w = lambda w: jnp.pad(w, ((0, 0), (0, 128 - N_HEADS))).astype(BF16)
            qkv = in_proj_conv(hn, w_in[:, :qkv_dim].astype(BF16), gdn_conv_w[layer])
            gate = matmul(hn, w_in[:, qkv_dim:qkvg].astype(BF16), F32)
            gcum, gcum_t, beta = gdn_gates(hn, pad_w(w_in[:, qkvg:qkvg + N_HEADS]), pad_w(w_in[:, qkvg + N_HEADS:]),
                                           gdn_a_log[layer], gdn_dt_bias[layer])
            o = gdn_core(qkv, gate, gcum, gcum_t, beta, gdn_o_norm_w[layer])
            h, hn = proj_resid_norm(o, gdn_w_out[layer].astype(BF16), h, ln_ffn_w[layer])
        else:
            j = layer - n_gdn
            q = matmul_heads(hn, sb_w_q[j].astype(BF16), scale=HEAD_DIM ** -0.5)
            o = sb_attention(q, kv)
            h, hn = proj_resid_norm(o, sb_w_out[j].astype(BF16), h, ln_ffn_w[layer])
        last = layer == depth - 1
        next_w = final_norm_w if last else ln_mix_w[layer + 1]
        h, hn = conv_ffn(hn, ffn_w_up[layer].astype(BF16), ffn_conv_w[layer], ffn_conv_b[layer],
                         ffn_w_down[layer].astype(BF16), h, next_w, F32 if last else BF16)
        if layer == n_gdn - 1:
            kv = matmul_heads(rmsnorm(h, kv_norm_w, BF16), sb_w_kv.astype(BF16))
    return hn[None]
```

```python
import functools

import jax
import jax.numpy as jnp
from jax import lax
from jax.experimental import pallas as pl
from jax.experimental.pallas import tpu as pltpu

F32 = jnp.float32
BF16 = jnp.bfloat16
EPS = 1e-6

D_MODEL = 2048
N_HEADS = 16
HEAD_DIM = 128
D_FF = 5632
GDN_CHUNK = 64
GDN_BLOCK = 256
GDN_HEADS_PER_STEP = 4
SB_BLOCK = 256
SB_HEADS_PER_STEP = 4
SB_LOG_SURV_FLOOR = -104.0
VMEM_LIMIT = 56 * 1024 * 1024


HALO = 16


def _row_halo(x, tm):
    s, d = x.shape
    tails = x.reshape(s // tm, tm, d)[:, tm - HALO:, :]
    return jnp.concatenate([jnp.zeros((1, HALO, d), x.dtype), tails[:-1]], axis=0)


def _params(sem):
    return pltpu.CompilerParams(dimension_semantics=sem, vmem_limit_bytes=VMEM_LIMIT)


def _dot(a, b):
    return jnp.dot(a, b, preferred_element_type=F32)


def _dot_nt(a, b):
    return lax.dot_general(a, b, (((1,), (1,)), ((), ())), preferred_element_type=F32)


def _dot_tn(a, b):
    return lax.dot_general(a, b, (((0,), (0,)), ((), ())), preferred_element_type=F32)


def _split3(a):
    hi = a.astype(BF16)
    r1 = a - hi.astype(F32)
    mid = r1.astype(BF16)
    lo = (r1 - mid.astype(F32)).astype(BF16)
    return hi, mid, lo


def _sigmoid(x):
    return 0.5 + 0.5 * jnp.tanh(0.5 * x)


def _silu(x):
    h = 0.5 * x
    return h + h * jnp.tanh(h)


def _softplus(x):
    return jnp.maximum(x, 0.0) + jnp.log(1.0 + jnp.exp(-jnp.abs(x)))


def _rms(x, w):
    return x * lax.rsqrt(jnp.mean(x * x, axis=-1, keepdims=True) + EPS) * w


def _rmsnorm_kernel(x_ref, w_ref, o_ref):
    o_ref[...] = _rms(x_ref[...], w_ref[...]).astype(o_ref.dtype)


def rmsnorm(x, w, out_dtype, tm=512):
    s, d = x.shape
    return pl.pallas_call(
        _rmsnorm_kernel,
        grid=(s // tm,),
        in_specs=[pl.BlockSpec((tm, d), lambda i: (i, 0)), pl.BlockSpec((1, d), lambda i: (0, 0))],
        out_specs=pl.BlockSpec((tm, d), lambda i: (i, 0)),
        out_shape=jax.ShapeDtypeStruct((s, d), out_dtype),
        compiler_params=_params(("parallel",)),
        name="rmsnorm",
    )(x, w.reshape(1, d))


def _matmul_kernel(x_ref, w_ref, o_ref):
    o_ref[...] = _dot(x_ref[...], w_ref[...]).astype(o_ref.dtype)


def matmul(x, w, out_dtype, tm=1024, tn=1024):
    s, k = x.shape
    n = w.shape[1]
    return pl.pallas_call(
        _matmul_kernel,
        grid=(s // tm, n // tn),
        in_specs=[pl.BlockSpec((tm, k), lambda i, j: (i, 0)), pl.BlockSpec((k, tn), lambda i, j: (0, j))],
        out_specs=pl.BlockSpec((tm, tn), lambda i, j: (i, j)),
        out_shape=jax.ShapeDtypeStruct((s, n), out_dtype),
        compiler_params=_params(("parallel", "parallel")),
        name="matmul",
    )(x, w)


def _matmul_heads_kernel(x_ref, w_ref, o_ref, *, scale, heads_per_tile):
    acc = _dot(x_ref[...], w_ref[...])
    if scale is not None:
        acc = acc * scale
    for t in range(heads_per_tile):
        o_ref[t] = acc[:, t * HEAD_DIM:(t + 1) * HEAD_DIM].astype(o_ref.dtype)


def matmul_heads(x, w, scale=None, tm=1024, heads_per_tile=4):
    s, k = x.shape
    nh = w.shape[1] // HEAD_DIM
    tn = heads_per_tile * HEAD_DIM
    return pl.pallas_call(
        functools.partial(_matmul_heads_kernel, scale=scale, heads_per_tile=heads_per_tile),
        grid=(s // tm, nh // heads_per_tile),
        in_specs=[pl.BlockSpec((tm, k), lambda i, j: (i, 0)), pl.BlockSpec((k, tn), lambda i, j: (0, j))],
        out_specs=pl.BlockSpec((heads_per_tile, tm, HEAD_DIM), lambda i, j: (j, i, 0)),
        out_shape=jax.ShapeDtypeStruct((nh, s, HEAD_DIM), BF16),
        compiler_params=_params(("parallel", "parallel")),
        name="matmul_heads",
    )(x, w)


def _proj_resid_norm_kernel(a_ref, w_ref, r_ref, nw_ref, h_ref, hn_ref):
    h = r_ref[...] + _dot(a_ref[...], w_ref[...])
    h_ref[...] = h
    hn_ref[...] = _rms(h, nw_ref[...]).astype(hn_ref.dtype)


def proj_resid_norm(a, w, resid, norm_w, tm=512):
    s, k = a.shape
    d = w.shape[1]
    return pl.pallas_call(
        _proj_resid_norm_kernel,
        grid=(s // tm,),
        in_specs=[pl.BlockSpec((tm, k), lambda i: (i, 0)), pl.BlockSpec((k, d), lambda i: (0, 0)),
                  pl.BlockSpec((tm, d), lambda i: (i, 0)), pl.BlockSpec((1, d), lambda i: (0, 0))],
        out_specs=[pl.BlockSpec((tm, d), lambda i: (i, 0)), pl.BlockSpec((tm, d), lambda i: (i, 0))],
        out_shape=[jax.ShapeDtypeStruct((s, d), F32), jax.ShapeDtypeStruct((s, d), BF16)],
        compiler_params=_params(("parallel",)),
        name="proj_resid_norm",
    )(a, w, resid, norm_w.reshape(1, d))


def _chunk_masks(tb, chunk):
    shift = chunk.bit_length() - 1
    row = lax.broadcasted_iota(jnp.int32, (tb, tb), 0)
    col = lax.broadcasted_iota(jnp.int32, (tb, tb), 1)
    same = lax.shift_right_logical(row, shift) == lax.shift_right_logical(col, shift)
    return jnp.logical_and(same, row >= col), jnp.logical_and(same, row > col)


def _gdn_gates_kernel(hn_ref, wa_ref, wb_ref, alog_ref, dtb_ref, g_ref, gt_ref, beta_ref, *, tb, chunk):
    hn = hn_ref[...]
    a = _dot(hn, wa_ref[...])
    b = _dot(hn, wb_ref[...])
    g = -jnp.exp(alog_ref[...]) * _softplus(a + dtb_ref[...])
    causal, _ = _chunk_masks(tb, chunk)
    lower = causal.astype(BF16)
    g_hi, g_mid, g_lo = _split3(g)
    gcum = _dot(lower, g_hi) + _dot(lower, g_mid) + _dot(lower, g_lo)
    g_ref[...] = gcum
    gt_ref[...] = gcum.T[:N_HEADS, :]
    beta_ref[...] = _sigmoid(b)


def gdn_gates(hn, w_a, w_b, a_log, dt_bias, tb=GDN_BLOCK, chunk=GDN_CHUNK):
    s, d = hn.shape
    pad = lambda v: jnp.pad(v.astype(F32), (0, 128 - N_HEADS)).reshape(1, 128)
    return pl.pallas_call(
        functools.partial(_gdn_gates_kernel, tb=tb, chunk=chunk),
        grid=(s // tb,),
        in_specs=[pl.BlockSpec((tb, d), lambda i: (i, 0)),
                  pl.BlockSpec((d, 128), lambda i: (0, 0)), pl.BlockSpec((d, 128), lambda i: (0, 0)),
                  pl.BlockSpec((1, 128), lambda i: (0, 0)), pl.BlockSpec((1, 128), lambda i: (0, 0))],
        out_specs=[pl.BlockSpec((tb, 128), lambda i: (i, 0)),
                   pl.BlockSpec((N_HEADS, tb), lambda i: (0, i)),
                   pl.BlockSpec((tb, 128), lambda i: (i, 0))],
        out_shape=[jax.ShapeDtypeStruct((s, 128), F32), jax.ShapeDtypeStruct((N_HEADS, s), F32),
                   jax.ShapeDtypeStruct((s, 128), F32)],
        compiler_params=_params(("parallel",)),
        name="gdn_gates",
    )(hn, w_a, w_b, pad(a_log), pad(dt_bias))


def _mm_solve(a, b):
    return _dot(a.astype(BF16), b.astype(BF16))


def _gdn_core_kernel(q_ref, k_ref, v_ref, pg_ref, g_ref, gt_ref, beta_ref, onw_ref, o_ref, state, *, tb, chunk, hb):
    hg = pl.program_id(0)
    t = pl.program_id(1)

    @pl.when(t == 0)
    def _():
        state[...] = jnp.zeros_like(state)

    gblk = g_ref[...]
    bblk = beta_ref[...]
    lane = lax.broadcasted_iota(jnp.int32, (tb, 128), 1)
    causal, strict = _chunk_masks(tb, chunk)
    levels = chunk.bit_length() - 2

    heads = range(hb)
    sls = [slice(hh * HEAD_DIM, (hh + 1) * HEAD_DIM) for hh in heads]
    q, k, v, g_col, b_col, g_row, e_g = [], [], [], [], [], [], []
    for hh in heads:
        head = hg * hb + hh
        q.append(q_ref[:, sls[hh]])
        k.append(k_ref[:, sls[hh]])
        v.append(v_ref[:, sls[hh]])
        g_col.append(jnp.sum(jnp.where(lane == head, gblk, 0.0), axis=1, keepdims=True))
        b_col.append(jnp.sum(jnp.where(lane == head, bblk, 0.0), axis=1, keepdims=True))
        g_row.append(gt_ref[pl.ds(head, 1), :])
        e_g.append(jnp.exp(g_col[hh]))

    k_bf = [k[hh].astype(BF16) for hh in heads]
    k_beta = [k[hh] * b_col[hh] for hh in heads]
    decay = [jnp.where(causal, jnp.exp(jnp.minimum(g_col[hh] - g_row[hh], 0.0)), 0.0) for hh in heads]
    a_mat = [_dot_nt(k_beta[hh].astype(BF16), k_bf[hh]) for hh in heads]
    qp = [jnp.where(strict, -(a_mat[hh] * decay[hh]), 0.0) for hh in heads]
    rhs = [jnp.concatenate([v[hh] * b_col[hh], k_beta[hh] * e_g[hh]], axis=1) for hh in heads]
    sol = [rhs[hh] + _mm_solve(qp[hh], rhs[hh]) for hh in heads]
    for _ in range(levels):
        qp = [_mm_solve(qp[hh], qp[hh]) for hh in heads]
        sol = [sol[hh] + _mm_solve(qp[hh], sol[hh]) for hh in heads]
    u = [sol[hh][:, :HEAD_DIM] for hh in heads]
    w_bf = [sol[hh][:, HEAD_DIM:].astype(BF16) for hh in heads]
    attn = [(_dot_nt(q[hh].astype(BF16), k_bf[hh]) * decay[hh]).astype(BF16) for hh in heads]
    q_dec_bf = [(q[hh] * e_g[hh]).astype(BF16) for hh in heads]

    s = [state[hh] for hh in heads]
    v_new = [[] for _ in heads]
    o_state = [[] for _ in heads]
    for j in range(tb // chunk):
        r = slice(j * chunk, (j + 1) * chunk)
        for hh in heads:
            s_bf = s[hh].astype(BF16)
            v_new_j = (u[hh][r] - _dot(w_bf[hh][r], s_bf)).astype(BF16)
            o_state[hh].append(_dot(q_dec_bf[hh][r], s_bf))
            g_last = g_col[hh][(j + 1) * chunk - 1:(j + 1) * chunk, :]
            k_dec = k[hh][r] * jnp.exp(g_last - g_col[hh][r])
            s[hh] = s[hh] * jnp.exp(g_last) + _dot_tn(k_dec.astype(BF16), v_new_j)
            v_new[hh].append(v_new_j)
    for hh in heads:
        state[hh] = s[hh]
        o = jnp.concatenate(o_state[hh], axis=0) + _dot(attn[hh], jnp.concatenate(v_new[hh], axis=0))
        gate = pg_ref[:, sls[hh]]
        o_ref[:, sls[hh]] = (_rms(o, onw_ref[...]) * _silu(gate)).astype(o_ref.dtype)


def gdn_core(qkv, gate, gcum, gcum_t, beta, o_norm_w, tb=GDN_BLOCK, chunk=GDN_CHUNK, hb=GDN_HEADS_PER_STEP):
    s = qkv.shape[0]
    w = hb * HEAD_DIM
    nhg = N_HEADS // hb
    pspec = lambda off: pl.BlockSpec((tb, w), lambda h, t, off=off: (t, off * nhg + h))
    return pl.pallas_call(
        functools.partial(_gdn_core_kernel, tb=tb, chunk=chunk, hb=hb),
        grid=(nhg, s // tb),
        in_specs=[pspec(0), pspec(1), pspec(2), pspec(0),
                  pl.BlockSpec((tb, 128), lambda h, t: (t, 0)),
                  pl.BlockSpec((N_HEADS, tb), lambda h, t: (0, t)),
                  pl.BlockSpec((tb, 128), lambda h, t: (t, 0)),
                  pl.BlockSpec((1, HEAD_DIM), lambda h, t: (0, 0))],
        out_specs=pl.BlockSpec((tb, w), lambda h, t: (t, h)),
        out_shape=jax.ShapeDtypeStruct((s, N_HEADS * HEAD_DIM), BF16),
        scratch_shapes=[pltpu.VMEM((hb, HEAD_DIM, HEAD_DIM), F32)],
        compiler_params=_params(("parallel", "arbitrary")),
        name="gdn_core",
    )(qkv, qkv, qkv, gate, gcum, gcum_t, beta, o_norm_w.reshape(1, HEAD_DIM))


def _in_proj_conv_kernel(x_ref, halo_ref, w_ref, cw_ref, o_ref, *u, tm, n_seg_tiles):
    j = pl.program_id(1)
    nsub = len(u)
    rows = tm // nsub
    w = w_ref[...]
    cw = cw_ref[...]
    is_qk = j < 2 * n_seg_tiles
    scale = jnp.where(j < n_seg_tiles, HEAD_DIM ** -0.5, 1.0).astype(F32)

    u[0][0:HALO, :] = _dot(halo_ref[0], w)
    for c in range(nsub):
        res = _dot(x_ref[c * rows:(c + 1) * rows, :], w)
        u[c][HALO:HALO + rows, :] = res
        if c + 1 < nsub:
            u[c + 1][0:HALO, :] = res[rows - HALO:, :]
    for c in range(nsub):
        uc = u[c][...]
        y = (cw[0:1, :] * pltpu.roll(uc, 3, 0)[HALO:, :] + cw[1:2, :] * pltpu.roll(uc, 2, 0)[HALO:, :]
             + cw[2:3, :] * pltpu.roll(uc, 1, 0)[HALO:, :] + cw[3:4, :] * uc[HALO:, :])
        y = _silu(y)
        for hd in range(y.shape[1] // HEAD_DIM):
            sl = slice(hd * HEAD_DIM, (hd + 1) * HEAD_DIM)
            yh = y[:, sl]
            inv = lax.rsqrt(jnp.sum(yh * yh, axis=-1, keepdims=True) + EPS) * scale
            o_ref[c * rows:(c + 1) * rows, sl] = yh * jnp.where(is_qk, inv, 1.0)


def in_proj_conv(x, w, conv_w, tm=1024, tn=1024, nsub=4):
    s, k = x.shape
    n = w.shape[1]
    n_seg_tiles = (n // 3) // tn
    return pl.pallas_call(
        functools.partial(_in_proj_conv_kernel, tm=tm, n_seg_tiles=n_seg_tiles),
        grid=(s // tm, n // tn),
        in_specs=[pl.BlockSpec((tm, k), lambda i, j: (i, 0)),
                  pl.BlockSpec((1, HALO, k), lambda i, j: (i, 0, 0)),
                  pl.BlockSpec((k, tn), lambda i, j: (0, j)),
                  pl.BlockSpec((4, tn), lambda i, j: (0, j))],
        out_specs=pl.BlockSpec((tm, tn), lambda i, j: (i, j)),
        out_shape=jax.ShapeDtypeStruct((s, n), F32),
        scratch_shapes=[pltpu.VMEM((tm // nsub + HALO, tn), F32) for _ in range(nsub)],
        compiler_params=_params(("parallel", "parallel")),
        name="in_proj_conv",
    )(x, _row_halo(x, tm), w, conv_w)


def _sb_kernel(q_ref, k_ref, v_ref, o_ref, acc, ls, *, blk, hb):
    i = pl.program_id(1)
    heads = range(hb)
    row = lax.broadcasted_iota(jnp.int32, (blk, blk), 0)
    col = lax.broadcasted_iota(jnp.int32, (blk, blk), 1)
    suffix = (row >= col).astype(BF16)
    valid = col < row
    acc[...] = jnp.zeros_like(acc)
    ls[...] = jnp.zeros_like(ls)

    def sweep(kb, diagonal):
        off = pl.multiple_of(kb * blk, blk)
        z = [_dot_nt(q_ref[hh], k_ref[hh, pl.ds(off, blk), :]) for hh in heads]
        log_1m = [-_softplus(z[hh]) for hh in heads]
        if diagonal:
            log_1m = [jnp.where(valid, log_1m[hh], 0.0) for hh in heads]
        parts = [_split3(log_1m[hh]) for hh in heads]
        incl = [_dot(parts[hh][0], suffix) + _dot(parts[hh][1], suffix) + _dot(parts[hh][2], suffix) for hh in heads]
        p = [jnp.exp(z[hh] + incl[hh] + ls[hh]) for hh in heads]
        if diagonal:
            p = [jnp.where(valid, p[hh], 0.0) for hh in heads]
        m = None
        for hh in heads:
            acc[hh] += _dot(p[hh].astype(BF16), v_ref[hh, pl.ds(off, blk), :])
            ls_new = ls[hh] + incl[hh][:, 0:1]
            ls[hh] = ls_new
            mh = jnp.max(ls_new)
            m = mh if m is None else jnp.maximum(m, mh)
        return m

    m0 = sweep(i, True)

    def cond(carry):
        kb, m = carry
        return jnp.logical_and(kb >= 0, m > SB_LOG_SURV_FLOOR)

    def body(carry):
        kb, _ = carry
        return kb - 1, sweep(kb, False)

    lax.while_loop(cond, body, (i - 1, m0))
    for hh in heads:
        o_ref[:, hh * HEAD_DIM:(hh + 1) * HEAD_DIM] = acc[hh].astype(o_ref.dtype)


def sb_attention(q, kv, blk=SB_BLOCK, hb=SB_HEADS_PER_STEP):
    nh, s, d = q.shape
    nhg = nh // hb
    return pl.pallas_call(
        functools.partial(_sb_kernel, blk=blk, hb=hb),
        grid=(nhg, s // blk),
        in_specs=[pl.BlockSpec((hb, blk, d), lambda h, i: (h, i, 0)),
                  pl.BlockSpec((hb, s, d), lambda h, i: (h, 0, 0), pipeline_mode=pl.Buffered(1)),
                  pl.BlockSpec((hb, s, d), lambda h, i: (nhg + h, 0, 0), pipeline_mode=pl.Buffered(1))],
        out_specs=pl.BlockSpec((blk, hb * d), lambda h, i: (i, h)),
        out_shape=jax.ShapeDtypeStruct((s, nh * d), BF16),
        scratch_shapes=[pltpu.VMEM((hb, blk, d), F32), pltpu.VMEM((hb, blk, 1), F32)],
        compiler_params=_params(("parallel", "arbitrary")),
        name="sb_attention",
    )(q, kv, kv)


def _ffn_kernel(x_ref, halo_ref, wg_ref, wv_ref, cwg_ref, cwv_ref, cbg_ref, cbv_ref, wd_ref, r_ref, nw_ref,
                h_ref, hn_ref, *scratch, tm, nj, nsub):
    j = pl.program_id(1)
    x_ext = scratch[-1]
    u = scratch[:-1]
    tfs = wg_ref.shape[1] // nsub

    @pl.when(j == 0)
    def _():
        h_ref[...] = r_ref[...]
        x_ext[0:HALO, :] = halo_ref[0]
        x_ext[HALO:HALO + tm, :] = x_ref[...]

    xe = x_ext[...]
    for c in range(nsub):
        cs = slice(c * tfs, (c + 1) * tfs)
        u[2 * c][...] = _dot(xe, wg_ref[:, cs])
        u[2 * c + 1][...] = _dot(xe, wv_ref[:, cs])

    def conv(uc, cw, cb):
        return (cw[0:1, :] * uc[HALO - 2:HALO - 2 + tm, :] + cw[1:2, :] * uc[HALO - 1:HALO - 1 + tm, :]
                + cw[2:3, :] * uc[HALO:HALO + tm, :] + cb)

    contrib = None
    for c in range(nsub):
        cs = slice(c * tfs, (c + 1) * tfs)
        yg = conv(u[2 * c], cwg_ref[:, cs], cbg_ref[:, cs])
        yv = conv(u[2 * c + 1], cwv_ref[:, cs], cbv_ref[:, cs])
        act = (_silu(yg) * yv).astype(BF16)
        part = _dot(act, wd_ref[cs, :])
        contrib = part if contrib is None else contrib + part
    h_ref[...] += contrib

    @pl.when(j == nj - 1)
    def _():
        hn_ref[...] = _rms(h_ref[...], nw_ref[...]).astype(hn_ref.dtype)


def conv_ffn(hn, w_up, conv_w, conv_b, w_down, resid, norm_w, norm_dtype, tm=512, tf=512, nsub=2):
    s, d = hn.shape
    nt = s // tm
    nj = D_FF // tf
    conv_b = conv_b.reshape(1, 2 * D_FF)
    return pl.pallas_call(
        functools.partial(_ffn_kernel, tm=tm, nj=nj, nsub=nsub),
        grid=(nt, nj),
        in_specs=[pl.BlockSpec((tm, d), lambda i, j: (i, 0)),
                  pl.BlockSpec((1, HALO, d), lambda i, j: (i, 0, 0)),
                  pl.BlockSpec((d, tf), lambda i, j: (0, j)),
                  pl.BlockSpec((d, tf), lambda i, j: (0, nj + j)),
                  pl.BlockSpec((3, tf), lambda i, j: (0, j)),
                  pl.BlockSpec((3, tf), lambda i, j: (0, nj + j)),
                  pl.BlockSpec((1, tf), lambda i, j: (0, j)),
                  pl.BlockSpec((1, tf), lambda i, j: (0, nj + j)),
                  pl.BlockSpec((tf, d), lambda i, j: (j, 0)),
                  pl.BlockSpec((tm, d), lambda i, j: (i, 0)),
                  pl.BlockSpec((1, d), lambda i, j: (0, 0))],
        out_specs=[pl.BlockSpec((tm, d), lambda i, j: (i, 0)), pl.BlockSpec((tm, d), lambda i, j: (i, 0))],
        out_shape=[jax.ShapeDtypeStruct((s, d), F32), jax.ShapeDtypeStruct((s, d), norm_dtype)],
        scratch_shapes=([pltpu.VMEM((tm + HALO, tf // nsub), F32) for _ in range(2 * nsub)]
                        + [pltpu.VMEM((tm + HALO, d), hn.dtype)]),
        compiler_params=_params(("parallel", "arbitrary")),
        name="conv_ffn",
    )(hn, _row_halo(hn, tm), w_up, w_up, conv_w, conv_w, conv_b, conv_b, w_down, resid, norm_w.reshape(1, d))


def kernel(x, ln_mix_w, ln_ffn_w, gdn_w_in, gdn_conv_w, gdn_a_log, gdn_dt_bias, gdn_o_norm_w, gdn_w_out,
           kv_norm_w, sb_w_kv, sb_w_q, sb_w_out, ffn_w_up, ffn_conv_w, ffn_conv_b, ffn_w_down, final_norm_w):
    depth = ln_mix_w.shape[0]
    n_gdn = gdn_w_in.shape[0]
    qkv_dim = 3 * N_HEADS * HEAD_DIM
    qkvg = 4 * N_HEADS * HEAD_DIM
    h = x[0]
    hn = rmsnorm(h, ln_mix_w[0], BF16)
    kv = None
    for layer in range(depth):
        if layer < n_gdn:
            w_in = gdn_w_in[layer]
            pad_w = lambda w: jnp.pad(w, ((0, 0), (0, 128 - N_HEADS))).astype(BF16)
            qkv = in_proj_conv(hn, w_in[:, :qkv_dim].astype(BF16), gdn_conv_w[layer])
            gate = matmul(hn, w_in[:, qkv_dim:qkvg].astype(BF16), F32)
            gcum, gcum_t, beta = gdn_gates(hn, pad_w(w_in[:, qkvg:qkvg + N_HEADS]), pad_w(w_in[:, qkvg + N_HEADS:]),
                                           gdn_a_log[layer], gdn_dt_bias[layer])
            o = gdn_core(qkv, gate, gcum, gcum_t, beta, gdn_o_norm_w[layer])
            h, hn = proj_resid_norm(o, gdn_w_out[layer].astype(BF16), h, ln_ffn_w[layer])
        else:
            j = layer - n_gdn
            q = matmul_heads(hn, sb_w_q[j].astype(BF16), scale=HEAD_DIM ** -0.5)
            o = sb_attention(q, kv)
            h, hn = proj_resid_norm(o, sb_w_out[j].astype(BF16), h, ln_ffn_w[layer])
        last = layer == depth - 1
        next_w = final_norm_w if last else ln_mix_w[layer + 1]
        h, hn = conv_ffn(hn, ffn_w_up[layer].astype(BF16), ffn_conv_w[layer], ffn_conv_b[layer],
                         ffn_w_down[layer].astype(BF16), h, next_w, F32 if last else BF16)
        if layer == n_gdn - 1:
            kv = matmul_heads(rmsnorm(h, kv_norm_w, BF16), sb_w_kv.astype(BF16))
    return hn[None]
```

```python
import functools

import jax
import jax.numpy as jnp
from jax import lax
from jax.experimental import pallas as pl
from jax.experimental.pallas import tpu as pltpu

F32 = jnp.float32
BF16 = jnp.bfloat16
EPS = 1e-6

D_MODEL = 2048
N_HEADS = 16
HEAD_DIM = 128
D_FF = 5632
GDN_CHUNK = 64
GDN_BLOCK = 256
GDN_HEADS_PER_STEP = 4
SB_BLOCK = 256
SB_HEADS_PER_STEP = 4
SB_LOG_SURV_FLOOR = -104.0
VMEM_LIMIT = 56 * 1024 * 1024


HALO = 16


def _row_halo(x, tm):
    s, d = x.shape
    tails = x.reshape(s // tm, tm, d)[:, tm - HALO:, :]
    return jnp.concatenate([jnp.zeros((1, HALO, d), x.dtype), tails[:-1]], axis=0)


def _params(sem):
    return pltpu.CompilerParams(dimension_semantics=sem, vmem_limit_bytes=VMEM_LIMIT)


def _dot(a, b):
    return jnp.dot(a, b, preferred_element_type=F32)


def _dot_nt(a, b):
    return lax.dot_general(a, b, (((1,), (1,)), ((), ())), preferred_element_type=F32)


def _dot_tn(a, b):
    return lax.dot_general(a, b, (((0,), (0,)), ((), ())), preferred_element_type=F32)


def _split3(a):
    hi = a.astype(BF16)
    r1 = a - hi.astype(F32)
    mid = r1.astype(BF16)
    lo = (r1 - mid.astype(F32)).astype(BF16)
    return hi, mid, lo


def _sigmoid(x):
    return 0.5 + 0.5 * jnp.tanh(0.5 * x)


def _silu(x):
    h = 0.5 * x
    return h + h * jnp.tanh(h)


def _softplus(x):
    return jnp.maximum(x, 0.0) + jnp.log(1.0 + jnp.exp(-jnp.abs(x)))


def _rms(x, w):
    return x * lax.rsqrt(jnp.mean(x * x, axis=-1, keepdims=True) + EPS) * w


def _rmsnorm_kernel(x_ref, w_ref, o_ref):
    o_ref[...] = _rms(x_ref[...], w_ref[...]).astype(o_ref.dtype)


def rmsnorm(x, w, out_dtype, tm=1024):
    s, d = x.shape
    return pl.pallas_call(
        _rmsnorm_kernel,
        grid=(s // tm,),
        in_specs=[pl.BlockSpec((tm, d), lambda i: (i, 0)), pl.BlockSpec((1, d), lambda i: (0, 0))],
        out_specs=pl.BlockSpec((tm, d), lambda i: (i, 0)),
        out_shape=jax.ShapeDtypeStruct((s, d), out_dtype),
        compiler_params=_params(("parallel",)),
        name="rmsnorm",
    )(x, w.reshape(1, d))


def _matmul_kernel(x_ref, w_ref, o_ref):
    o_ref[...] = _dot(x_ref[...], w_ref[...]).astype(o_ref.dtype)


def matmul(x, w, out_dtype, tm=1024, tn=1024):
    s, k = x.shape
    n = w.shape[1]
    return pl.pallas_call(
        _matmul_kernel,
        grid=(s // tm, n // tn),
        in_specs=[pl.BlockSpec((tm, k), lambda i, j: (i, 0)), pl.BlockSpec((k, tn), lambda i, j: (0, j))],
        out_specs=pl.BlockSpec((tm, tn), lambda i, j: (i, j)),
        out_shape=jax.ShapeDtypeStruct((s, n), out_dtype),
        compiler_params=_params(("parallel", "parallel")),
        name="matmul",
    )(x, w)


def _matmul_heads_kernel(x_ref, w_ref, o_ref, *, scale, heads_per_tile):
    acc = _dot(x_ref[...], w_ref[...])
    if scale is not None:
        acc = acc * scale
    for t in range(heads_per_tile):
        o_ref[t] = acc[:, t * HEAD_DIM:(t + 1) * HEAD_DIM].astype(o_ref.dtype)


def matmul_heads(x, w, scale=None, tm=1024, heads_per_tile=8):
    s, k = x.shape
    nh = w.shape[1] // HEAD_DIM
    tn = heads_per_tile * HEAD_DIM
    return pl.pallas_call(
        functools.partial(_matmul_heads_kernel, scale=scale, heads_per_tile=heads_per_tile),
        grid=(s // tm, nh // heads_per_tile),
        in_specs=[pl.BlockSpec((tm, k), lambda i, j: (i, 0)), pl.BlockSpec((k, tn), lambda i, j: (0, j))],
        out_specs=pl.BlockSpec((heads_per_tile, tm, HEAD_DIM), lambda i, j: (j, i, 0)),
        out_shape=jax.ShapeDtypeStruct((nh, s, HEAD_DIM), BF16),
        compiler_params=_params(("parallel", "parallel")),
        name="matmul_heads",
    )(x, w)


def _proj_resid_norm_kernel(a_ref, w_ref, r_ref, nw_ref, h_ref, hn_ref):
    h = r_ref[...] + _dot(a_ref[...], w_ref[...])
    h_ref[...] = h
    hn_ref[...] = _rms(h, nw_ref[...]).astype(hn_ref.dtype)


def proj_resid_norm(a, w, resid, norm_w, tm=512):
    s, k = a.shape
    d = w.shape[1]
    return pl.pallas_call(
        _proj_resid_norm_kernel,
        grid=(s // tm,),
        in_specs=[pl.BlockSpec((tm, k), lambda i: (i, 0)), pl.BlockSpec((k, d), lambda i: (0, 0)),
                  pl.BlockSpec((tm, d), lambda i: (i, 0)), pl.BlockSpec((1, d), lambda i: (0, 0))],
        out_specs=[pl.BlockSpec((tm, d), lambda i: (i, 0)), pl.BlockSpec((tm, d), lambda i: (i, 0))],
        out_shape=[jax.ShapeDtypeStruct((s, d), F32), jax.ShapeDtypeStruct((s, d), BF16)],
        compiler_params=_params(("parallel",)),
        name="proj_resid_norm",
    )(a, w, resid, norm_w.reshape(1, d))


def _chunk_masks(tb, chunk):
    shift = chunk.bit_length() - 1
    row = lax.broadcasted_iota(jnp.int32, (tb, tb), 0)
    col = lax.broadcasted_iota(jnp.int32, (tb, tb), 1)
    same = lax.shift_right_logical(row, shift) == lax.shift_right_logical(col, shift)
    return jnp.logical_and(same, row >= col), jnp.logical_and(same, row > col)


def _gdn_gates_kernel(hn_ref, wa_ref, wb_ref, alog_ref, dtb_ref, g_ref, gt_ref, beta_ref, *, tb, chunk):
    hn = hn_ref[...]
    a = _dot(hn, wa_ref[...])
    b = _dot(hn, wb_ref[...])
    g = -jnp.exp(alog_ref[...]) * _softplus(a + dtb_ref[...])
    causal, _ = _chunk_masks(tb, chunk)
    lower = causal.astype(BF16)
    g_hi, g_mid, g_lo = _split3(g)
    gcum = _dot(lower, g_hi) + _dot(lower, g_mid) + _dot(lower, g_lo)
    g_ref[...] = gcum
    gt_ref[...] = gcum.T[:N_HEADS, :]
    beta_ref[...] = _sigmoid(b)


def gdn_gates(hn, w_a, w_b, a_log, dt_bias, tb=GDN_BLOCK, chunk=GDN_CHUNK):
    s, d = hn.shape
    pad = lambda v: jnp.pad(v.astype(F32), (0, 128 - N_HEADS)).reshape(1, 128)
    return pl.pallas_call(
        functools.partial(_gdn_gates_kernel, tb=tb, chunk=chunk),
        grid=(s // tb,),
        in_specs=[pl.BlockSpec((tb, d), lambda i: (i, 0)),
                  pl.BlockSpec((d, 128), lambda i: (0, 0)), pl.BlockSpec((d, 128), lambda i: (0, 0)),
                  pl.BlockSpec((1, 128), lambda i: (0, 0)), pl.BlockSpec((1, 128), lambda i: (0, 0))],
        out_specs=[pl.BlockSpec((tb, 128), lambda i: (i, 0)),
                   pl.BlockSpec((N_HEADS, tb), lambda i: (0, i)),
                   pl.BlockSpec((tb, 128), lambda i: (i, 0))],
        out_shape=[jax.ShapeDtypeStruct((s, 128), F32), jax.ShapeDtypeStruct((N_HEADS, s), F32),
                   jax.ShapeDtypeStruct((s, 128), F32)],
        compiler_params=_params(("parallel",)),
        name="gdn_gates",
    )(hn, w_a, w_b, pad(a_log), pad(dt_bias))


def _mm_solve(a, b):
    return _dot(a.astype(BF16), b.astype(BF16))


def _gdn_core_kernel(q_ref, k_ref, v_ref, pg_ref, g_ref, gt_ref, beta_ref, onw_ref, o_ref, state, *, tb, chunk, hb):
    hg = pl.program_id(0)
    t = pl.program_id(1)

    @pl.when(t == 0)
    def _():
        state[...] = jnp.zeros_like(state)

    gblk = g_ref[...]
    bblk = beta_ref[...]
    lane = lax.broadcasted_iota(jnp.int32, (tb, 128), 1)
    causal, strict = _chunk_masks(tb, chunk)
    levels = chunk.bit_length() - 2

    heads = range(hb)
    sls = [slice(hh * HEAD_DIM, (hh + 1) * HEAD_DIM) for hh in heads]
    q, k, v, g_col, b_col, g_row, e_g = [], [], [], [], [], [], []
    for hh in heads:
        head = hg * hb + hh
        q.append(q_ref[:, sls[hh]])
        k.append(k_ref[:, sls[hh]])
        v.append(v_ref[:, sls[hh]])
        g_col.append(jnp.sum(jnp.where(lane == head, gblk, 0.0), axis=1, keepdims=True))
        b_col.append(jnp.sum(jnp.where(lane == head, bblk, 0.0), axis=1, keepdims=True))
        g_row.append(gt_ref[pl.ds(head, 1), :])
        e_g.append(jnp.exp(g_col[hh]))

    k_bf = [k[hh].astype(BF16) for hh in heads]
    k_beta = [k[hh] * b_col[hh] for hh in heads]
    decay = [jnp.where(causal, jnp.exp(jnp.minimum(g_col[hh] - g_row[hh], 0.0)), 0.0) for hh in heads]
    a_mat = [_dot_nt(k_beta[hh].astype(BF16), k_bf[hh]) for hh in heads]
    qp = [jnp.where(strict, -(a_mat[hh] * decay[hh]), 0.0) for hh in heads]
    rhs = [jnp.concatenate([v[hh] * b_col[hh], k_beta[hh] * e_g[hh]], axis=1) for hh in heads]
    sol = [rhs[hh] + _mm_solve(qp[hh], rhs[hh]) for hh in heads]
    for _ in range(levels):
        qp = [_mm_solve(qp[hh], qp[hh]) for hh in heads]
        sol = [sol[hh] + _mm_solve(qp[hh], sol[hh]) for hh in heads]
    u = [sol[hh][:, :HEAD_DIM] for hh in heads]
    w_bf = [sol[hh][:, HEAD_DIM:].astype(BF16) for hh in heads]
    attn = [(_dot_nt(q[hh].astype(BF16), k_bf[hh]) * decay[hh]).astype(BF16) for hh in heads]
    q_dec_bf = [(q[hh] * e_g[hh]).astype(BF16) for hh in heads]

    s = [state[hh] for hh in heads]
    v_new = [[] for _ in heads]
    o_state = [[] for _ in heads]
    for j in range(tb // chunk):
        r = slice(j * chunk, (j + 1) * chunk)
        for hh in heads:
            s_bf = s[hh].astype(BF16)
            v_new_j = (u[hh][r] - _dot(w_bf[hh][r], s_bf)).astype(BF16)
            o_state[hh].append(_dot(q_dec_bf[hh][r], s_bf))
            g_last = g_col[hh][(j + 1) * chunk - 1:(j + 1) * chunk, :]
            k_dec = k[hh][r] * jnp.exp(g_last - g_col[hh][r])
            s[hh] = s[hh] * jnp.exp(g_last) + _dot_tn(k_dec.astype(BF16), v_new_j)
            v_new[hh].append(v_new_j)
    for hh in heads:
        state[hh] = s[hh]
        o = jnp.concatenate(o_state[hh], axis=0) + _dot(attn[hh], jnp.concatenate(v_new[hh], axis=0))
        gate = pg_ref[:, sls[hh]]
        o_ref[:, sls[hh]] = (_rms(o, onw_ref[...]) * _silu(gate)).astype(o_ref.dtype)


def gdn_core(qkv, gate, gcum, gcum_t, beta, o_norm_w, tb=GDN_BLOCK, chunk=GDN_CHUNK, hb=GDN_HEADS_PER_STEP):
    s = qkv.shape[0]
    w = hb * HEAD_DIM
    nhg = N_HEADS // hb
    pspec = lambda off: pl.BlockSpec((tb, w), lambda h, t, off=off: (t, off * nhg + h))
    return pl.pallas_call(
        functools.partial(_gdn_core_kernel, tb=tb, chunk=chunk, hb=hb),
        grid=(nhg, s // tb),
        in_specs=[pspec(0), pspec(1), pspec(2), pspec(0),
                  pl.BlockSpec((tb, 128), lambda h, t: (t, 0)),
                  pl.BlockSpec((N_HEADS, tb), lambda h, t: (0, t)),
                  pl.BlockSpec((tb, 128), lambda h, t: (t, 0)),
                  pl.BlockSpec((1, HEAD_DIM), lambda h, t: (0, 0))],
        out_specs=pl.BlockSpec((tb, w), lambda h, t: (t, h)),
        out_shape=jax.ShapeDtypeStruct((s, N_HEADS * HEAD_DIM), BF16),
        scratch_shapes=[pltpu.VMEM((hb, HEAD_DIM, HEAD_DIM), F32)],
        compiler_params=_params(("parallel", "arbitrary")),
        name="gdn_core",
    )(qkv, qkv, qkv, gate, gcum, gcum_t, beta, o_norm_w.reshape(1, HEAD_DIM))


def _in_proj_conv_kernel(x_ref, halo_ref, w_ref, cw_ref, o_ref, *u, tm, n_seg_tiles):
    j = pl.program_id(1)
    nsub = len(u)
    rows = tm // nsub
    w = w_ref[...]
    cw = cw_ref[...]
    is_qk = j < 2 * n_seg_tiles
    scale = jnp.where(j < n_seg_tiles, HEAD_DIM ** -0.5, 1.0).astype(F32)

    u[0][0:HALO, :] = _dot(halo_ref[0], w)
    for c in range(nsub):
        res = _dot(x_ref[c * rows:(c + 1) * rows, :], w)
        u[c][HALO:HALO + rows, :] = res
        if c + 1 < nsub:
            u[c + 1][0:HALO, :] = res[rows - HALO:, :]
    for c in range(nsub):
        uc = u[c][...]
        y = (cw[0:1, :] * pltpu.roll(uc, 3, 0)[HALO:, :] + cw[1:2, :] * pltpu.roll(uc, 2, 0)[HALO:, :]
             + cw[2:3, :] * pltpu.roll(uc, 1, 0)[HALO:, :] + cw[3:4, :] * uc[HALO:, :])
        y = _silu(y)
        for hd in range(y.shape[1] // HEAD_DIM):
            sl = slice(hd * HEAD_DIM, (hd + 1) * HEAD_DIM)
            yh = y[:, sl]
            inv = lax.rsqrt(jnp.sum(yh * yh, axis=-1, keepdims=True) + EPS) * scale
            o_ref[c * rows:(c + 1) * rows, sl] = yh * jnp.where(is_qk, inv, 1.0)


def in_proj_conv(x, w, conv_w, tm=1024, tn=1024, nsub=4):
    s, k = x.shape
    n = w.shape[1]
    n_seg_tiles = (n // 3) // tn
    return pl.pallas_call(
        functools.partial(_in_proj_conv_kernel, tm=tm, n_seg_tiles=n_seg_tiles),
        grid=(s // tm, n // tn),
        in_specs=[pl.BlockSpec((tm, k), lambda i, j: (i, 0)),
                  pl.BlockSpec((1, HALO, k), lambda i, j: (i, 0, 0)),
                  pl.BlockSpec((k, tn), lambda i, j: (0, j)),
                  pl.BlockSpec((4, tn), lambda i, j: (0, j))],
        out_specs=pl.BlockSpec((tm, tn), lambda i, j: (i, j)),
        out_shape=jax.ShapeDtypeStruct((s, n), F32),
        scratch_shapes=[pltpu.VMEM((tm // nsub + HALO, tn), F32) for _ in range(nsub)],
        compiler_params=_params(("parallel", "parallel")),
        name="in_proj_conv",
    )(x, _row_halo(x, tm), w, conv_w)


def _sb_kernel(q_ref, k_ref, v_ref, o_ref, acc, ls, *, blk, hb):
    i = pl.program_id(1)
    heads = range(hb)
    row = lax.broadcasted_iota(jnp.int32, (blk, blk), 0)
    col = lax.broadcasted_iota(jnp.int32, (blk, blk), 1)
    suffix = (row >= col).astype(BF16)
    valid = col < row
    acc[...] = jnp.zeros_like(acc)
    ls[...] = jnp.zeros_like(ls)

    def sweep(kb, diagonal):
        off = pl.multiple_of(kb * blk, blk)
        z = [_dot_nt(q_ref[hh], k_ref[hh, pl.ds(off, blk), :]) for hh in heads]
        log_1m = [-_softplus(z[hh]) for hh in heads]
        if diagonal:
            log_1m = [jnp.where(valid, log_1m[hh], 0.0) for hh in heads]
        parts = [_split3(log_1m[hh]) for hh in heads]
        incl = [_dot(parts[hh][0], suffix) + _dot(parts[hh][1], suffix) + _dot(parts[hh][2], suffix) for hh in heads]
        p = [jnp.exp(z[hh] + incl[hh] + ls[hh]) for hh in heads]
        if diagonal:
            p = [jnp.where(valid, p[hh], 0.0) for hh in heads]
        m = None
        for hh in heads:
            acc[hh] += _dot(p[hh].astype(BF16), v_ref[hh, pl.ds(off, blk), :])
            ls_new = ls[hh] + incl[hh][:, 0:1]
            ls[hh] = ls_new
            mh = jnp.max(ls_new)
            m = mh if m is None else jnp.maximum(m, mh)
        return m

    m0 = sweep(i, True)

    def cond(carry):
        kb, m = carry
        return jnp.logical_and(kb >= 0, m > SB_LOG_SURV_FLOOR)

    def body(carry):
        kb, _ = carry
        return kb - 1, sweep(kb, False)

    lax.while_loop(cond, body, (i - 1, m0))
    for hh in heads:
        o_ref[:, hh * HEAD_DIM:(hh + 1) * HEAD_DIM] = acc[hh].astype(o_ref.dtype)


def sb_attention(q, kv, blk=SB_BLOCK, hb=SB_HEADS_PER_STEP):
    nh, s, d = q.shape
    nhg = nh // hb
    return pl.pallas_call(
        functools.partial(_sb_kernel, blk=blk, hb=hb),
        grid=(nhg, s // blk),
        in_specs=[pl.BlockSpec((hb, blk, d), lambda h, i: (h, i, 0)),
                  pl.BlockSpec((hb, s, d), lambda h, i: (h, 0, 0), pipeline_mode=pl.Buffered(1)),
                  pl.BlockSpec((hb, s, d), lambda h, i: (nhg + h, 0, 0), pipeline_mode=pl.Buffered(1))],
        out_specs=pl.BlockSpec((blk, hb * d), lambda h, i: (i, h)),
        out_shape=jax.ShapeDtypeStruct((s, nh * d), BF16),
        scratch_shapes=[pltpu.VMEM((hb, blk, d), F32), pltpu.VMEM((hb, blk, 1), F32)],
        compiler_params=_params(("parallel", "arbitrary")),
        name="sb_attention",
    )(q, kv, kv)


def _ffn_kernel(x_ref, halo_ref, wg_ref, wv_ref, cwg_ref, cwv_ref, cbg_ref, cbv_ref, wd_ref, r_ref, nw_ref,
                h_ref, hn_ref, *scratch, tm, nj, nsub):
    j = pl.program_id(1)
    x_ext = scratch[-1]
    u = scratch[:-1]
    tfs = wg_ref.shape[1] // nsub

    @pl.when(j == 0)
    def _():
        h_ref[...] = r_ref[...]
        x_ext[0:HALO, :] = halo_ref[0]
        x_ext[HALO:HALO + tm, :] = x_ref[...]

    xe = x_ext[...]
    for c in range(nsub):
        cs = slice(c * tfs, (c + 1) * tfs)
        u[2 * c][...] = _dot(xe, wg_ref[:, cs])
        u[2 * c + 1][...] = _dot(xe, wv_ref[:, cs])

    def conv(uc, cw, cb):
        return (cw[0:1, :] * uc[HALO - 2:HALO - 2 + tm, :] + cw[1:2, :] * uc[HALO - 1:HALO - 1 + tm, :]
                + cw[2:3, :] * uc[HALO:HALO + tm, :] + cb)

    contrib = None
    for c in range(nsub):
        cs = slice(c * tfs, (c + 1) * tfs)
        yg = conv(u[2 * c], cwg_ref[:, cs], cbg_ref[:, cs])
        yv = conv(u[2 * c + 1], cwv_ref[:, cs], cbv_ref[:, cs])
        act = (_silu(yg) * yv).astype(BF16)
        part = _dot(act, wd_ref[cs, :])
        contrib = part if contrib is None else contrib + part
    h_ref[...] += contrib

    @pl.when(j == nj - 1)
    def _():
        hn_ref[...] = _rms(h_ref[...], nw_ref[...]).astype(hn_ref.dtype)


def conv_ffn(hn, w_up, conv_w, conv_b, w_down, resid, norm_w, norm_dtype, tm=512, tf=512, nsub=2):
    s, d = hn.shape
    nt = s // tm
    nj = D_FF // tf
    conv_b = conv_b.reshape(1, 2 * D_FF)
    return pl.pallas_call(
        functools.partial(_ffn_kernel, tm=tm, nj=nj, nsub=nsub),
        grid=(nt, nj),
        in_specs=[pl.BlockSpec((tm, d), lambda i, j: (i, 0)),
                  pl.BlockSpec((1, HALO, d), lambda i, j: (i, 0, 0)),
                  pl.BlockSpec((d, tf), lambda i, j: (0, j)),
                  pl.BlockSpec((d, tf), lambda i, j: (0, nj + j)),
                  pl.BlockSpec((3, tf), lambda i, j: (0, j)),
                  pl.BlockSpec((3, tf), lambda i, j: (0, nj + j)),
                  pl.BlockSpec((1, tf), lambda i, j: (0, j)),
                  pl.BlockSpec((1, tf), lambda i, j: (0, nj + j)),
                  pl.BlockSpec((tf, d), lambda i, j: (j, 0)),
                  pl.BlockSpec((tm, d), lambda i, j: (i, 0)),
                  pl.BlockSpec((1, d), lambda i, j: (0, 0))],
        out_specs=[pl.BlockSpec((tm, d), lambda i, j: (i, 0)), pl.BlockSpec((tm, d), lambda i, j: (i, 0))],
        out_shape=[jax.ShapeDtypeStruct((s, d), F32), jax.ShapeDtypeStruct((s, d), norm_dtype)],
        scratch_shapes=([pltpu.VMEM((tm + HALO, tf // nsub), F32) for _ in range(2 * nsub)]
                        + [pltpu.VMEM((tm + HALO, d), hn.dtype)]),
        compiler_params=_params(("parallel", "arbitrary")),
        name="conv_ffn",
    )(hn, _row_halo(hn, tm), w_up, w_up, conv_w, conv_w, conv_b, conv_b, w_down, resid, norm_w.reshape(1, d))


def kernel(x, ln_mix_w, ln_ffn_w, gdn_w_in, gdn_conv_w, gdn_a_log, gdn_dt_bias, gdn_o_norm_w, gdn_w_out,
           kv_norm_w, sb_w_kv, sb_w_q, sb_w_out, ffn_w_up, ffn_conv_w, ffn_conv_b, ffn_w_down, final_norm_w):
    depth = ln_mix_w.shape[0]
    n_gdn = gdn_w_in.shape[0]
    qkv_dim = 3 * N_HEADS * HEAD_DIM
    qkvg = 4 * N_HEADS * HEAD_DIM
    h = x[0]
    hn = rmsnorm(h, ln_mix_w[0], BF16)
    kv = None
    for layer in range(depth):
        if layer < n_gdn:
            w_in = gdn_w_in[layer]
            pad_w = lambda w: jnp.pad(w, ((0, 0), (0, 128 - N_HEADS))).astype(BF16)
            qkv = in_proj_conv(hn, w_in[:, :qkv_dim].astype(BF16), gdn_conv_w[layer])
            gate = matmul(hn, w_in[:, qkv_dim:qkvg].astype(BF16), F32)
            gcum, gcum_t, beta = gdn_gates(hn, pad_w(w_in[:, qkvg:qkvg + N_HEADS]), pad_w(w_in[:, qkvg + N_HEADS:]),
                                           gdn_a_log[layer], gdn_dt_bias[layer])
            o = gdn_core(qkv, gate, gcum, gcum_t, beta, gdn_o_norm_w[layer])
            h, hn = proj_resid_norm(o, gdn_w_out[layer].astype(BF16), h, ln_ffn_w[layer])
        else:
            j = layer - n_gdn
            q = matmul_heads(hn, sb_w_q[j].astype(BF16), scale=HEAD_DIM ** -0.5)
            o = sb_attention(q, kv)
            h, hn = proj_resid_norm(o, sb_w_out[j].astype(BF16), h, ln_ffn_w[layer])
        last = layer == depth - 1
        next_w = final_norm_w if last else ln_mix_w[layer + 1]
        h, hn = conv_ffn(hn, ffn_w_up[layer].astype(BF16), ffn_conv_w[layer], ffn_conv_b[layer],
                         ffn_w_down[layer].astype(BF16), h, next_w, F32 if last else BF16)
        if layer == n_gdn - 1:
            kv = matmul_heads(rmsnorm(h, kv_norm_w, BF16), sb_w_kv.astype(BF16))
    return hn[None]
```

```python
import functools

import jax
import jax.numpy as jnp
from jax import lax
from jax.experimental import pallas as pl
from jax.experimental.pallas import tpu as pltpu

F32 = jnp.float32
BF16 = jnp.bfloat16
EPS = 1e-6

D_MODEL = 2048
N_HEADS = 16
HEAD_DIM = 128
D_FF = 5632
GDN_CHUNK = 64
GDN_BLOCK = 256
GDN_HEADS_PER_STEP = 4
SB_BLOCK = 256
SB_HEADS_PER_STEP = 4
SB_LOG_SURV_FLOOR = -104.0
VMEM_LIMIT = 56 * 1024 * 1024


HALO = 16


def _row_halo(x, tm):
    s, d = x.shape
    tails = x.reshape(s // tm, tm, d)[:, tm - HALO:, :]
    return jnp.concatenate([jnp.zeros((1, HALO, d), x.dtype), tails[:-1]], axis=0)


def _params(sem):
    return pltpu.CompilerParams(dimension_semantics=sem, vmem_limit_bytes=VMEM_LIMIT)


def _dot(a, b):
    return jnp.dot(a, b, preferred_element_type=F32)


def _dot_nt(a, b):
    return lax.dot_general(a, b, (((1,), (1,)), ((), ())), preferred_element_type=F32)


def _dot_tn(a, b):
    return lax.dot_general(a, b, (((0,), (0,)), ((), ())), preferred_element_type=F32)


def _split3(a):
    hi = a.astype(BF16)
    r1 = a - hi.astype(F32)
    mid = r1.astype(BF16)
    lo = (r1 - mid.astype(F32)).astype(BF16)
    return hi, mid, lo


def _sigmoid(x):
    return 0.5 + 0.5 * jnp.tanh(0.5 * x)


def _silu(x):
    h = 0.5 * x
    return h + h * jnp.tanh(h)


def _softplus(x):
    return jnp.maximum(x, 0.0) + jnp.log(1.0 + jnp.exp(-jnp.abs(x)))


def _rms(x, w):
    return x * lax.rsqrt(jnp.mean(x * x, axis=-1, keepdims=True) + EPS) * w


def _rmsnorm_kernel(x_ref, w_ref, o_ref):
    o_ref[...] = _rms(x_ref[...], w_ref[...]).astype(o_ref.dtype)


def rmsnorm(x, w, out_dtype, tm=1024):
    s, d = x.shape
    return pl.pallas_call(
        _rmsnorm_kernel,
        grid=(s // tm,),
        in_specs=[pl.BlockSpec((tm, d), lambda i: (i, 0)), pl.BlockSpec((1, d), lambda i: (0, 0))],
        out_specs=pl.BlockSpec((tm, d), lambda i: (i, 0)),
        out_shape=jax.ShapeDtypeStruct((s, d), out_dtype),
        compiler_params=_params(("parallel",)),
        name="rmsnorm",
    )(x, w.reshape(1, d))


def _matmul_kernel(x_ref, w_ref, o_ref):
    o_ref[...] = _dot(x_ref[...], w_ref[...]).astype(o_ref.dtype)


def matmul(x, w, out_dtype, tm=1024, tn=1024):
    s, k = x.shape
    n = w.shape[1]
    return pl.pallas_call(
        _matmul_kernel,
        grid=(s // tm, n // tn),
        in_specs=[pl.BlockSpec((tm, k), lambda i, j: (i, 0)), pl.BlockSpec((k, tn), lambda i, j: (0, j))],
        out_specs=pl.BlockSpec((tm, tn), lambda i, j: (i, j)),
        out_shape=jax.ShapeDtypeStruct((s, n), out_dtype),
        compiler_params=_params(("parallel", "parallel")),
        name="matmul",
    )(x, w)


def _matmul_heads_kernel(x_ref, w_ref, o_ref, *, scale, heads_per_tile):
    acc = _dot(x_ref[...], w_ref[...])
    if scale is not None:
        acc = acc * scale
    for t in range(heads_per_tile):
        o_ref[t] = acc[:, t * HEAD_DIM:(t + 1) * HEAD_DIM].astype(o_ref.dtype)


def matmul_heads(x, w, scale=None, tm=1024, heads_per_tile=8):
    s, k = x.shape
    nh = w.shape[1] // HEAD_DIM
    tn = heads_per_tile * HEAD_DIM
    return pl.pallas_call(
        functools.partial(_matmul_heads_kernel, scale=scale, heads_per_tile=heads_per_tile),
        grid=(s // tm, nh // heads_per_tile),
        in_specs=[pl.BlockSpec((tm, k), lambda i, j: (i, 0)), pl.BlockSpec((k, tn), lambda i, j: (0, j))],
        out_specs=pl.BlockSpec((heads_per_tile, tm, HEAD_DIM), lambda i, j: (j, i, 0)),
        out_shape=jax.ShapeDtypeStruct((nh, s, HEAD_DIM), BF16),
        compiler_params=_params(("parallel", "parallel")),
        name="matmul_heads",
    )(x, w)


def _norm_matmul_heads_kernel(h_ref, nw_ref, w_ref, o_ref, xn, *, heads_per_tile):
    @pl.when(pl.program_id(1) == 0)
    def _():
        xn[...] = _rms(h_ref[...], nw_ref[...]).astype(xn.dtype)

    acc = _dot(xn[...], w_ref[...])
    for t in range(heads_per_tile):
        o_ref[t] = acc[:, t * HEAD_DIM:(t + 1) * HEAD_DIM].astype(o_ref.dtype)


def norm_matmul_heads(h, norm_w, w, tm=1024, heads_per_tile=8):
    s, k = h.shape
    nh = w.shape[1] // HEAD_DIM
    tn = heads_per_tile * HEAD_DIM
    return pl.pallas_call(
        functools.partial(_norm_matmul_heads_kernel, heads_per_tile=heads_per_tile),
        grid=(s // tm, nh // heads_per_tile),
        in_specs=[pl.BlockSpec((tm, k), lambda i, j: (i, 0)), pl.BlockSpec((1, k), lambda i, j: (0, 0)),
                  pl.BlockSpec((k, tn), lambda i, j: (0, j))],
        out_specs=pl.BlockSpec((heads_per_tile, tm, HEAD_DIM), lambda i, j: (j, i, 0)),
        out_shape=jax.ShapeDtypeStruct((nh, s, HEAD_DIM), BF16),
        scratch_shapes=[pltpu.VMEM((tm, k), BF16)],
        compiler_params=_params(("parallel", "arbitrary")),
        name="norm_matmul_heads",
    )(h, norm_w.reshape(1, k), w)


def _proj_resid_norm_kernel(a_ref, w_ref, r_ref, nw_ref, h_ref, hn_ref):
    h = r_ref[...] + _dot(a_ref[...], w_ref[...])
    h_ref[...] = h
    hn_ref[...] = _rms(h, nw_ref[...]).astype(hn_ref.dtype)


def proj_resid_norm(a, w, resid, norm_w, tm=512):
    s, k = a.shape
    d = w.shape[1]
    return pl.pallas_call(
        _proj_resid_norm_kernel,
        grid=(s // tm,),
        in_specs=[pl.BlockSpec((tm, k), lambda i: (i, 0)), pl.BlockSpec((k, d), lambda i: (0, 0)),
                  pl.BlockSpec((tm, d), lambda i: (i, 0)), pl.BlockSpec((1, d), lambda i: (0, 0))],
        out_specs=[pl.BlockSpec((tm, d), lambda i: (i, 0)), pl.BlockSpec((tm, d), lambda i: (i, 0))],
        out_shape=[jax.ShapeDtypeStruct((s, d), F32), jax.ShapeDtypeStruct((s, d), BF16)],
        compiler_params=_params(("parallel",)),
        name="proj_resid_norm",
    )(a, w, resid, norm_w.reshape(1, d))


def _chunk_masks(tb, chunk):
    shift = chunk.bit_length() - 1
    row = lax.broadcasted_iota(jnp.int32, (tb, tb), 0)
    col = lax.broadcasted_iota(jnp.int32, (tb, tb), 1)
    same = lax.shift_right_logical(row, shift) == lax.shift_right_logical(col, shift)
    return jnp.logical_and(same, row >= col), jnp.logical_and(same, row > col)


def _gdn_gates_kernel(hn_ref, wa_ref, wb_ref, alog_ref, dtb_ref, g_ref, gt_ref, beta_ref, *, tb, chunk):
    hn = hn_ref[...]
    a = _dot(hn, wa_ref[...])
    b = _dot(hn, wb_ref[...])
    g = -jnp.exp(alog_ref[...]) * _softplus(a + dtb_ref[...])
    causal, _ = _chunk_masks(tb, chunk)
    lower = causal.astype(BF16)
    g_hi, g_mid, g_lo = _split3(g)
    gcum = _dot(lower, g_hi) + _dot(lower, g_mid) + _dot(lower, g_lo)
    g_ref[...] = gcum
    gt_ref[...] = gcum.T[:N_HEADS, :]
    beta_ref[...] = _sigmoid(b)


def gdn_gates(hn, w_a, w_b, a_log, dt_bias, tb=GDN_BLOCK, chunk=GDN_CHUNK):
    s, d = hn.shape
    pad = lambda v: jnp.pad(v.astype(F32), (0, 128 - N_HEADS)).reshape(1, 128)
    return pl.pallas_call(
        functools.partial(_gdn_gates_kernel, tb=tb, chunk=chunk),
        grid=(s // tb,),
        in_specs=[pl.BlockSpec((tb, d), lambda i: (i, 0)),
                  pl.BlockSpec((d, 128), lambda i: (0, 0)), pl.BlockSpec((d, 128), lambda i: (0, 0)),
                  pl.BlockSpec((1, 128), lambda i: (0, 0)), pl.BlockSpec((1, 128), lambda i: (0, 0))],
        out_specs=[pl.BlockSpec((tb, 128), lambda i: (i, 0)),
                   pl.BlockSpec((N_HEADS, tb), lambda i: (0, i)),
                   pl.BlockSpec((tb, 128), lambda i: (i, 0))],
        out_shape=[jax.ShapeDtypeStruct((s, 128), F32), jax.ShapeDtypeStruct((N_HEADS, s), F32),
                   jax.ShapeDtypeStruct((s, 128), F32)],
        compiler_params=_params(("parallel",)),
        name="gdn_gates",
    )(hn, w_a, w_b, pad(a_log), pad(dt_bias))


def _mm_solve(a, b):
    return _dot(a.astype(BF16), b.astype(BF16))


def _gdn_core_kernel(q_ref, k_ref, v_ref, pg_ref, g_ref, gt_ref, beta_ref, onw_ref, o_ref, state, *, tb, chunk, hb):
    hg = pl.program_id(0)
    t = pl.program_id(1)

    @pl.when(t == 0)
    def _():
        state[...] = jnp.zeros_like(state)

    gblk = g_ref[...]
    bblk = beta_ref[...]
    lane = lax.broadcasted_iota(jnp.int32, (tb, 128), 1)
    causal, strict = _chunk_masks(tb, chunk)
    levels = chunk.bit_length() - 2

    heads = range(hb)
    sls = [slice(hh * HEAD_DIM, (hh + 1) * HEAD_DIM) for hh in heads]
    q, k, v, g_col, b_col, g_row, e_g = [], [], [], [], [], [], []
    for hh in heads:
        head = hg * hb + hh
        q.append(q_ref[:, sls[hh]])
        k.append(k_ref[:, sls[hh]])
        v.append(v_ref[:, sls[hh]])
        g_col.append(jnp.sum(jnp.where(lane == head, gblk, 0.0), axis=1, keepdims=True))
        b_col.append(jnp.sum(jnp.where(lane == head, bblk, 0.0), axis=1, keepdims=True))
        g_row.append(gt_ref[pl.ds(head, 1), :])
        e_g.append(jnp.exp(g_col[hh]))

    k_bf = [k[hh].astype(BF16) for hh in heads]
    k_beta = [k[hh] * b_col[hh] for hh in heads]
    decay = [jnp.where(causal, jnp.exp(jnp.minimum(g_col[hh] - g_row[hh], 0.0)), 0.0) for hh in heads]
    a_mat = [_dot_nt(k_beta[hh].astype(BF16), k_bf[hh]) for hh in heads]
    qp = [jnp.where(strict, -(a_mat[hh] * decay[hh]), 0.0) for hh in heads]
    rhs = [jnp.concatenate([v[hh] * b_col[hh], k_beta[hh] * e_g[hh]], axis=1) for hh in heads]
    sol = [rhs[hh] + _mm_solve(qp[hh], rhs[hh]) for hh in heads]
    for _ in range(levels):
        qp = [_mm_solve(qp[hh], qp[hh]) for hh in heads]
        sol = [sol[hh] + _mm_solve(qp[hh], sol[hh]) for hh in heads]
    u = [sol[hh][:, :HEAD_DIM] for hh in heads]
    w_bf = [sol[hh][:, HEAD_DIM:].astype(BF16) for hh in heads]
    attn = [(_dot_nt(q[hh].astype(BF16), k_bf[hh]) * decay[hh]).astype(BF16) for hh in heads]
    q_dec_bf = [(q[hh] * e_g[hh]).astype(BF16) for hh in heads]

    s = [state[hh] for hh in heads]
    v_new = [[] for _ in heads]
    o_state = [[] for _ in heads]
    for j in range(tb // chunk):
        r = slice(j * chunk, (j + 1) * chunk)
        for hh in heads:
            s_bf = s[hh].astype(BF16)
            v_new_j = (u[hh][r] - _dot(w_bf[hh][r], s_bf)).astype(BF16)
            o_state[hh].append(_dot(q_dec_bf[hh][r], s_bf))
            g_last = g_col[hh][(j + 1) * chunk - 1:(j + 1) * chunk, :]
            k_dec = k[hh][r] * jnp.exp(g_last - g_col[hh][r])
            s[hh] = s[hh] * jnp.exp(g_last) + _dot_tn(k_dec.astype(BF16), v_new_j)
            v_new[hh].append(v_new_j)
    for hh in heads:
        state[hh] = s[hh]
        o = jnp.concatenate(o_state[hh], axis=0) + _dot(attn[hh], jnp.concatenate(v_new[hh], axis=0))
        gate = pg_ref[:, sls[hh]]
        o_ref[:, sls[hh]] = (_rms(o, onw_ref[...]) * _silu(gate)).astype(o_ref.dtype)


def gdn_core(qkv, gate, gcum, gcum_t, beta, o_norm_w, tb=GDN_BLOCK, chunk=GDN_CHUNK, hb=GDN_HEADS_PER_STEP):
    s = qkv.shape[0]
    w = hb * HEAD_DIM
    nhg = N_HEADS // hb
    pspec = lambda off: pl.BlockSpec((tb, w), lambda h, t, off=off: (t, off * nhg + h))
    return pl.pallas_call(
        functools.partial(_gdn_core_kernel, tb=tb, chunk=chunk, hb=hb),
        grid=(nhg, s // tb),
        in_specs=[pspec(0), pspec(1), pspec(2), pspec(0),
                  pl.BlockSpec((tb, 128), lambda h, t: (t, 0)),
                  pl.BlockSpec((N_HEADS, tb), lambda h, t: (0, t)),
                  pl.BlockSpec((tb, 128), lambda h, t: (t, 0)),
                  pl.BlockSpec((1, HEAD_DIM), lambda h, t: (0, 0))],
        out_specs=pl.BlockSpec((tb, w), lambda h, t: (t, h)),
        out_shape=jax.ShapeDtypeStruct((s, N_HEADS * HEAD_DIM), BF16),
        scratch_shapes=[pltpu.VMEM((hb, HEAD_DIM, HEAD_DIM), F32)],
        compiler_params=_params(("parallel", "arbitrary")),
        name="gdn_core",
    )(qkv, qkv, qkv, gate, gcum, gcum_t, beta, o_norm_w.reshape(1, HEAD_DIM))


def _in_proj_conv_kernel(x_ref, halo_ref, w_ref, cw_ref, o_ref, *u, tm, n_seg_tiles):
    j = pl.program_id(1)
    nsub = len(u)
    rows = tm // nsub
    w = w_ref[...]
    cw = cw_ref[...]
    is_qk = j < 2 * n_seg_tiles
    scale = jnp.where(j < n_seg_tiles, HEAD_DIM ** -0.5, 1.0).astype(F32)

    u[0][0:HALO, :] = _dot(halo_ref[0], w)
    for c in range(nsub):
        res = _dot(x_ref[c * rows:(c + 1) * rows, :], w)
        u[c][HALO:HALO + rows, :] = res
        if c + 1 < nsub:
            u[c + 1][0:HALO, :] = res[rows - HALO:, :]
    for c in range(nsub):
        uc = u[c][...]
        y = (cw[0:1, :] * pltpu.roll(uc, 3, 0)[HALO:, :] + cw[1:2, :] * pltpu.roll(uc, 2, 0)[HALO:, :]
             + cw[2:3, :] * pltpu.roll(uc, 1, 0)[HALO:, :] + cw[3:4, :] * uc[HALO:, :])
        y = _silu(y)
        for hd in range(y.shape[1] // HEAD_DIM):
            sl = slice(hd * HEAD_DIM, (hd + 1) * HEAD_DIM)
            yh = y[:, sl]
            inv = lax.rsqrt(jnp.sum(yh * yh, axis=-1, keepdims=True) + EPS) * scale
            o_ref[c * rows:(c + 1) * rows, sl] = yh * jnp.where(is_qk, inv, 1.0)


def in_proj_conv(x, w, conv_w, tm=1024, tn=1024, nsub=4):
    s, k = x.shape
    n = w.shape[1]
    n_seg_tiles = (n // 3) // tn
    return pl.pallas_call(
        functools.partial(_in_proj_conv_kernel, tm=tm, n_seg_tiles=n_seg_tiles),
        grid=(s // tm, n // tn),
        in_specs=[pl.BlockSpec((tm, k), lambda i, j: (i, 0)),
                  pl.BlockSpec((1, HALO, k), lambda i, j: (i, 0, 0)),
                  pl.BlockSpec((k, tn), lambda i, j: (0, j)),
                  pl.BlockSpec((4, tn), lambda i, j: (0, j))],
        out_specs=pl.BlockSpec((tm, tn), lambda i, j: (i, j)),
        out_shape=jax.ShapeDtypeStruct((s, n), F32),
        scratch_shapes=[pltpu.VMEM((tm // nsub + HALO, tn), F32) for _ in range(nsub)],
        compiler_params=_params(("parallel", "parallel")),
        name="in_proj_conv",
    )(x, _row_halo(x, tm), w, conv_w)


def _sb_kernel(q_ref, k_ref, v_ref, o_ref, acc, ls, *, blk, hb):
    i = pl.program_id(1)
    heads = range(hb)
    row = lax.broadcasted_iota(jnp.int32, (blk, blk), 0)
    col = lax.broadcasted_iota(jnp.int32, (blk, blk), 1)
    suffix = (row >= col).astype(BF16)
    valid = col < row
    acc[...] = jnp.zeros_like(acc)
    ls[...] = jnp.zeros_like(ls)

    def sweep(kb, diagonal):
        off = pl.multiple_of(kb * blk, blk)
        z = [_dot_nt(q_ref[hh], k_ref[hh, pl.ds(off, blk), :]) for hh in heads]
        log_1m = [-_softplus(z[hh]) for hh in heads]
        if diagonal:
            log_1m = [jnp.where(valid, log_1m[hh], 0.0) for hh in heads]
        parts = [_split3(log_1m[hh]) for hh in heads]
        incl = [_dot(parts[hh][0], suffix) + _dot(parts[hh][1], suffix) + _dot(parts[hh][2], suffix) for hh in heads]
        p = [jnp.exp(z[hh] + incl[hh] + ls[hh]) for hh in heads]
        if diagonal:
            p = [jnp.where(valid, p[hh], 0.0) for hh in heads]
        m = None
        for hh in heads:
            acc[hh] += _dot(p[hh].astype(BF16), v_ref[hh, pl.ds(off, blk), :])
            ls_new = ls[hh] + incl[hh][:, 0:1]
            ls[hh] = ls_new
            mh = jnp.max(ls_new)
            m = mh if m is None else jnp.maximum(m, mh)
        return m

    m0 = sweep(i, True)

    def cond(carry):
        kb, m = carry
        return jnp.logical_and(kb >= 0, m > SB_LOG_SURV_FLOOR)

    def body(carry):
        kb, _ = carry
        return kb - 1, sweep(kb, False)

    lax.while_loop(cond, body, (i - 1, m0))
    for hh in heads:
        o_ref[:, hh * HEAD_DIM:(hh + 1) * HEAD_DIM] = acc[hh].astype(o_ref.dtype)


def sb_attention(q, kv, blk=SB_BLOCK, hb=SB_HEADS_PER_STEP):
    nh, s, d = q.shape
    nhg = nh // hb
    return pl.pallas_call(
        functools.partial(_sb_kernel, blk=blk, hb=hb),
        grid=(nhg, s // blk),
        in_specs=[pl.BlockSpec((hb, blk, d), lambda h, i: (h, i, 0)),
                  pl.BlockSpec((hb, s, d), lambda h, i: (h, 0, 0), pipeline_mode=pl.Buffered(1)),
                  pl.BlockSpec((hb, s, d), lambda h, i: (nhg + h, 0, 0), pipeline_mode=pl.Buffered(1))],
        out_specs=pl.BlockSpec((blk, hb * d), lambda h, i: (i, h)),
        out_shape=jax.ShapeDtypeStruct((s, nh * d), BF16),
        scratch_shapes=[pltpu.VMEM((hb, blk, d), F32), pltpu.VMEM((hb, blk, 1), F32)],
        compiler_params=_params(("parallel", "arbitrary")),
        name="sb_attention",
    )(q, kv, kv)


def _ffn_kernel(x_ref, halo_ref, wg_ref, wv_ref, cwg_ref, cwv_ref, cbg_ref, cbv_ref, wd_ref, r_ref, nw_ref,
                h_ref, hn_ref, *scratch, tm, nj, nsub):
    j = pl.program_id(1)
    x_ext = scratch[-1]
    u = scratch[:-1]
    tfs = wg_ref.shape[1] // nsub

    @pl.when(j == 0)
    def _():
        h_ref[...] = r_ref[...]
        x_ext[0:HALO, :] = halo_ref[0]
        x_ext[HALO:HALO + tm, :] = x_ref[...]

    xe = x_ext[...]
    for c in range(nsub):
        cs = slice(c * tfs, (c + 1) * tfs)
        u[2 * c][...] = _dot(xe, wg_ref[:, cs])
        u[2 * c + 1][...] = _dot(xe, wv_ref[:, cs])

    def conv(uc, cw, cb):
        return (cw[0:1, :] * uc[HALO - 2:HALO - 2 + tm, :] + cw[1:2, :] * uc[HALO - 1:HALO - 1 + tm, :]
                + cw[2:3, :] * uc[HALO:HALO + tm, :] + cb)

    contrib = None
    for c in range(nsub):
        cs = slice(c * tfs, (c + 1) * tfs)
        yg = conv(u[2 * c], cwg_ref[:, cs], cbg_ref[:, cs])
        yv = conv(u[2 * c + 1], cwv_ref[:, cs], cbv_ref[:, cs])
        act = (_silu(yg) * yv).astype(BF16)
        part = _dot(act, wd_ref[cs, :])
        contrib = part if contrib is None else contrib + part
    h_ref[...] += contrib

    @pl.when(j == nj - 1)
    def _():
        hn_ref[...] = _rms(h_ref[...], nw_ref[...]).astype(hn_ref.dtype)


def conv_ffn(hn, w_up, conv_w, conv_b, w_down, resid, norm_w, norm_dtype, tm=512, tf=512, nsub=2):
    s, d = hn.shape
    nt = s // tm
    nj = D_FF // tf
    conv_b = conv_b.reshape(1, 2 * D_FF)
    return pl.pallas_call(
        functools.partial(_ffn_kernel, tm=tm, nj=nj, nsub=nsub),
        grid=(nt, nj),
        in_specs=[pl.BlockSpec((tm, d), lambda i, j: (i, 0)),
                  pl.BlockSpec((1, HALO, d), lambda i, j: (i, 0, 0)),
                  pl.BlockSpec((d, tf), lambda i, j: (0, j)),
                  pl.BlockSpec((d, tf), lambda i, j: (0, nj + j)),
                  pl.BlockSpec((3, tf), lambda i, j: (0, j)),
                  pl.BlockSpec((3, tf), lambda i, j: (0, nj + j)),
                  pl.BlockSpec((1, tf), lambda i, j: (0, j)),
                  pl.BlockSpec((1, tf), lambda i, j: (0, nj + j)),
                  pl.BlockSpec((tf, d), lambda i, j: (j, 0)),
                  pl.BlockSpec((tm, d), lambda i, j: (i, 0)),
                  pl.BlockSpec((1, d), lambda i, j: (0, 0))],
        out_specs=[pl.BlockSpec((tm, d), lambda i, j: (i, 0)), pl.BlockSpec((tm, d), lambda i, j: (i, 0))],
        out_shape=[jax.ShapeDtypeStruct((s, d), F32), jax.ShapeDtypeStruct((s, d), norm_dtype)],
        scratch_shapes=([pltpu.VMEM((tm + HALO, tf // nsub), F32) for _ in range(2 * nsub)]
                        + [pltpu.VMEM((tm + HALO, d), hn.dtype)]),
        compiler_params=_params(("parallel", "arbitrary")),
        name="conv_ffn",
    )(hn, _row_halo(hn, tm), w_up, w_up, conv_w, conv_w, conv_b, conv_b, w_down, resid, norm_w.reshape(1, d))


def kernel(x, ln_mix_w, ln_ffn_w, gdn_w_in, gdn_conv_w, gdn_a_log, gdn_dt_bias, gdn_o_norm_w, gdn_w_out,
           kv_norm_w, sb_w_kv, sb_w_q, sb_w_out, ffn_w_up, ffn_conv_w, ffn_conv_b, ffn_w_down, final_norm_w):
    depth = ln_mix_w.shape[0]
    n_gdn = gdn_w_in.shape[0]
    qkv_dim = 3 * N_HEADS * HEAD_DIM
    qkvg = 4 * N_HEADS * HEAD_DIM
    h = x[0]
    hn = rmsnorm(h, ln_mix_w[0], BF16)
    kv = None
    for layer in range(depth):
        if layer < n_gdn:
            w_in = gdn_w_in[layer]
            pad_w = lambda w: jnp.pad(w, ((0, 0), (0, 128 - N_HEADS))).astype(BF16)
            qkv = in_proj_conv(hn, w_in[:, :qkv_dim].astype(BF16), gdn_conv_w[layer])
            gate = matmul(hn, w_in[:, qkv_dim:qkvg].astype(BF16), F32)
            gcum, gcum_t, beta = gdn_gates(hn, pad_w(w_in[:, qkvg:qkvg + N_HEADS]), pad_w(w_in[:, qkvg + N_HEADS:]),
                                           gdn_a_log[layer], gdn_dt_bias[layer])
            o = gdn_core(qkv, gate, gcum, gcum_t, beta, gdn_o_norm_w[layer])
            h, hn = proj_resid_norm(o, gdn_w_out[layer].astype(BF16), h, ln_ffn_w[layer])
        else:
            j = layer - n_gdn
            q = matmul_heads(hn, sb_w_q[j].astype(BF16), scale=HEAD_DIM ** -0.5)
            o = sb_attention(q, kv)
            h, hn = proj_resid_norm(o, sb_w_out[j].astype(BF16), h, ln_ffn_w[layer])
        last = layer == depth - 1
        next_w = final_norm_w if last else ln_mix_w[layer + 1]
        h, hn = conv_ffn(hn, ffn_w_up[layer].astype(BF16), ffn_conv_w[layer], ffn_conv_b[layer],
                         ffn_w_down[layer].astype(BF16), h, next_w, F32 if last else BF16)
        if layer == n_gdn - 1:
            kv = norm_matmul_heads(h, kv_norm_w, sb_w_kv.astype(BF16))
    return hn[None]
```
